```python
import jax, jax.numpy as jnp
from jax import lax
import numpy as np

D_MODEL = 1024
BATCH = 2
SEQ = 8192
DEPTH = 1

MIX_WIDTH = D_MODEL
CONV_WIDTH = D_MODEL // 2
CONV_GROUPS = 8
CONV_GROUP_DIM = CONV_WIDTH // CONV_GROUPS
CONV_KSIZE = 31
GMLP_WIDTH = MIX_WIDTH - CONV_WIDTH
GMLP_HEADS = 8
GMLP_HEAD_DIM = GMLP_WIDTH // GMLP_HEADS
CHUNK = 128
N_GROUPS = 4
EXPERTS_PER_GROUP = 8
N_EXPERTS = N_GROUPS * EXPERTS_PER_GROUP
TOP_K_INNER = 2
D_EXPERT = D_MODEL // 4
N_MOD = 6
EPS = 1e-6

kernel_name = "hybrid_conv_gmlp_hmoe_block"


def rms_norm(x, g):
    xf = x.astype(jnp.float32)
    y = xf * lax.rsqrt(jnp.mean(xf * xf, axis=-1, keepdims=True) + EPS)
    return (y * g.astype(jnp.float32)).astype(x.dtype)


def layer_norm(x, g, b):
    xf = x.astype(jnp.float32)
    mu = jnp.mean(xf, axis=-1, keepdims=True)
    var = jnp.mean(jnp.square(xf - mu), axis=-1, keepdims=True)
    y = (xf - mu) * lax.rsqrt(var + EPS)
    return (y * g.astype(jnp.float32) + b.astype(jnp.float32)).astype(x.dtype)


def modulate(h, shift, scale):
    return h * (1 + scale[:, None, :]) + shift[:, None, :]


def conv_module(z, w_dw, b_dw, ln_g, ln_b):
    a, g = jnp.split(z, 2, axis=-1)
    y = a * jax.nn.sigmoid(g)
    y = lax.conv_general_dilated(
        y, w_dw[:, None, :].astype(y.dtype), window_strides=(1,),
        padding=((CONV_KSIZE - 1, 0),),
        dimension_numbers=("NWC", "WIO", "NWC"),
        feature_group_count=CONV_WIDTH) + b_dw
    B, S, _ = y.shape
    y = layer_norm(y.reshape(B, S, CONV_GROUPS, CONV_GROUP_DIM), ln_g, ln_b)
    return jax.nn.silu(y.reshape(B, S, CONV_WIDTH))


def chunked_sgu(z, ln_g, ln_b, w_s, b_s):
    u, v = jnp.split(jax.nn.gelu(z), 2, axis=-1)
    B, S, _ = u.shape
    v = v.reshape(B, S // CHUNK, CHUNK, GMLP_HEADS, GMLP_HEAD_DIM)
    v = layer_norm(v, ln_g, ln_b)
    w = w_s * jnp.tril(jnp.ones((CHUNK, CHUNK), dtype=w_s.dtype))
    mixed = jnp.einsum("hts,bcshd->bcthd", w, v) + b_s.T[None, None, :, :, None]
    return u * mixed.reshape(B, S, GMLP_WIDTH)


def hier_moe(h, w_coarse, b_coarse, w_fine, b_fine, w_gate, w_up, w_down):
    B, S, D = h.shape
    t = h.reshape(B * S, D)
    coarse_p = jax.nn.softmax((t @ w_coarse + b_coarse).astype(jnp.float32), axis=-1)
    grp = jnp.argmax(coarse_p, axis=-1)
    p_grp = jnp.max(coarse_p, axis=-1)
    fine_logits = (jnp.einsum("td,gde->tge", t, w_fine) + b_fine).astype(jnp.float32)
    fine_sel = jnp.take_along_axis(fine_logits, grp[:, None, None], axis=1)[:, 0]
    fine_p = jax.nn.softmax(fine_sel, axis=-1)
    top_v, top_i = lax.top_k(fine_p, TOP_K_INNER)
    top_v = top_v / jnp.sum(top_v, axis=-1, keepdims=True)
    inner = jnp.sum(jax.nn.one_hot(top_i, EXPERTS_PER_GROUP, dtype=jnp.float32) * top_v[..., None], axis=1)
    gates = (jax.nn.one_hot(grp, N_GROUPS, dtype=jnp.float32)[:, :, None]
             * (p_grp[:, None, None] * inner[:, None, :])).reshape(B * S, N_EXPERTS).astype(h.dtype)
    y = jnp.zeros_like(t)
    for e in range(N_EXPERTS):
        hid = jax.nn.silu(t @ w_gate[e]) * (t @ w_up[e])
        y = y + gates[:, e:e + 1] * (hid @ w_down[e])
    return y.reshape(B, S, D)


def setup_inputs(seed: int = 0) -> dict:
    key = jax.random.key(seed)
    ks = jax.random.split(key, 24)
    f32 = jnp.float32
    L, D = DEPTH, D_MODEL
    n = lambda k, shape, s: jax.random.normal(k, shape, f32) * s
    return {
        "x": n(ks[0], (BATCH, SEQ, D), 1.0),
        "c": n(ks[1], (BATCH, D), 1.0),
        "w_ada": n(ks[2], (L, D, N_MOD * D), 0.5 * D ** -0.5),
        "b_ada": n(ks[3], (L, N_MOD * D), 0.01),
        "g_mix": 1.0 + n(ks[4], (L, D), 0.02),
        "w_in": n(ks[5], (L, D, 2 * MIX_WIDTH), D ** -0.5),
        "conv_w": n(ks[6], (L, CONV_KSIZE, CONV_WIDTH), CONV_KSIZE ** -0.5),
        "conv_b": n(ks[7], (L, CONV_WIDTH), 0.01),
        "conv_ln_g": 1.0 + n(ks[8], (L, CONV_GROUPS, CONV_GROUP_DIM), 0.02),
        "conv_ln_b": n(ks[9], (L, CONV_GROUPS, CONV_GROUP_DIM), 0.01),
        "sgu_ln_g": 1.0 + n(ks[10], (L, GMLP_HEADS, GMLP_HEAD_DIM), 0.02),
        "sgu_ln_b": n(ks[11], (L, GMLP_HEADS, GMLP_HEAD_DIM), 0.01),
        "sgu_w": n(ks[12], (L, GMLP_HEADS, CHUNK, CHUNK), CHUNK ** -0.5),
        "sgu_b": 1.0 + n(ks[13], (L, GMLP_HEADS, CHUNK), 0.1),
        "w_out": n(ks[14], (L, MIX_WIDTH, D), MIX_WIDTH ** -0.5),
        "g_ffn": 1.0 + n(ks[15], (L, D), 0.02),
        "w_coarse": n(ks[16], (L, D, N_GROUPS), D ** -0.5),
        "b_coarse": n(ks[17], (L, N_GROUPS), 0.01),
        "w_fine": n(ks[18], (L, N_GROUPS, D, EXPERTS_PER_GROUP), D ** -0.5),
        "b_fine": n(ks[19], (L, N_GROUPS, EXPERTS_PER_GROUP), 0.01),
        "w_gate": n(ks[20], (L, N_EXPERTS, D, D_EXPERT), D ** -0.5),
        "w_up": n(ks[21], (L, N_EXPERTS, D, D_EXPERT), D ** -0.5),
        "w_down": n(ks[22], (L, N_EXPERTS, D_EXPERT, D), D_EXPERT ** -0.5),
        "g_final": 1.0 + n(ks[23], (D,), 0.02),
    }


def reference(x, c, w_ada, b_ada, g_mix, w_in, conv_w, conv_b, conv_ln_g, conv_ln_b,
              sgu_ln_g, sgu_ln_b, sgu_w, sgu_b, w_out, g_ffn, w_coarse, b_coarse,
              w_fine, b_fine, w_gate, w_up, w_down, g_final):
    c_act = jax.nn.silu(c)
    for l in range(DEPTH):
        mod = c_act @ w_ada[l] + b_ada[l]
        sh1, sc1, gt1, sh2, sc2, gt2 = jnp.split(mod, N_MOD, axis=-1)
        h = modulate(rms_norm(x, g_mix[l]), sh1, sc1)
        z = h @ w_in[l]
        z_conv = z[..., :2 * CONV_WIDTH]
        z_sgu = z[..., 2 * CONV_WIDTH:]
        y_conv = conv_module(z_conv, conv_w[l], conv_b[l], conv_ln_g[l], conv_ln_b[l])
        y_sgu = chunked_sgu(z_sgu, sgu_ln_g[l], sgu_ln_b[l], sgu_w[l], sgu_b[l])
        mix = jnp.concatenate([y_conv, y_sgu], axis=-1) @ w_out[l]
        x = x + gt1[:, None, :] * mix
        h = modulate(rms_norm(x, g_ffn[l]), sh2, sc2)
        x = x + gt2[:, None, :] * hier_moe(h, w_coarse[l], b_coarse[l], w_fine[l], b_fine[l],
                                           w_gate[l], w_up[l], w_down[l])
    return rms_norm(x, g_final)
```

```python
import functools

import jax
import jax.numpy as jnp
from jax import lax
from jax.experimental import pallas as pl
from jax.experimental.pallas import tpu as pltpu

F32 = jnp.float32
BF16 = jnp.bfloat16

D_MODEL = 1024
CONV_WIDTH = 512
CONV_KSIZE = 31
GMLP_WIDTH = 512
HEAD_DIM = 64
CHUNK = 128
N_GROUPS = 4
EXPERTS_PER_GROUP = 8
D_EXPERT = 256
EPS = 1e-6

LANES = 128
TM = 512
HALO = 32
RC = 128
TB = 256
TS = 512
TC = 256
ROW_W = D_MODEL + LANES
LANE_GRP = 8
LANE_RANK = 9
VMEM_LIMIT = 56 * 1024 * 1024


def _dot(a, b):
    return jnp.dot(a, b, preferred_element_type=F32)


def _sigmoid(x):
    return 0.5 * jnp.tanh(0.5 * x) + 0.5


def _gelu_tanh(x):
    c = 0.7978845608028654
    return 0.5 * x * (1.0 + jnp.tanh(c * (x + 0.044715 * (x * x * x))))


def _ada_kernel(c_ref, w_ref, b_ref, o_ref):
    c = c_ref[...]
    ca = c * _sigmoid(c)
    o_ref[...] = jnp.dot(ca, w_ref[...], preferred_element_type=F32,
                         precision=lax.Precision.HIGHEST) + b_ref[...]


def _ada(c_pad, w_ada, b_ada):
    n = w_ada.shape[1]
    tn = 1024
    return pl.pallas_call(
        _ada_kernel,
        out_shape=jax.ShapeDtypeStruct((8, n), F32),
        grid=(n // tn,),
        in_specs=[
            pl.BlockSpec((8, D_MODEL), lambda j: (0, 0)),
            pl.BlockSpec((D_MODEL, tn), lambda j: (0, j)),
            pl.BlockSpec((1, tn), lambda j: (0, j)),
        ],
        out_specs=pl.BlockSpec((8, tn), lambda j: (0, j)),
        name="ada",
    )(c_pad, w_ada, b_ada)


def _group_layer_norm(v, a64, g, b):
    mu = _dot(v.astype(BF16), a64)
    d = v - mu
    var = _dot((d * d).astype(BF16), a64)
    return d * lax.rsqrt(var + EPS) * g + b


def _mixer_kernel(x_ref, mod_ref, gmix_ref, win_ref, convw_ref, convb_ref, clng_ref, clnb_ref,
                  slng_ref, slnb_ref, a64_ref, w2_ref, bfull_ref, wout_ref, gffn_ref,
                  wrh_ref, wrl_ref, br_ref, ltri_ref,
                  x1_ref, hx_ref, cnt_ref,
                  ybuf, cat, carry):
    b = pl.program_id(0)
    i = pl.program_id(1)

    @pl.when(i == 0)
    def _():
        ybuf[0:HALO, :] = jnp.zeros((HALO, CONV_WIDTH), F32)

    @pl.when((b == 0) & (i == 0))
    def _():
        carry[...] = jnp.zeros((1, LANES), F32)

    x = x_ref[0]
    mod = mod_ref[0]
    sh1, sc1, gt1 = mod[0:1], mod[1:2], mod[2:3]
    sh2, sc2, gt2 = mod[3:4], mod[4:5], mod[5:6]
    del gt2

    ms = jnp.mean(x * x, axis=-1, keepdims=True)
    h = (x * lax.rsqrt(ms + EPS) * gmix_ref[...]) * (1.0 + sc1) + sh1
    hb = h.astype(BF16)
    a64 = a64_ref[...]

    ga = _dot(hb, win_ref[:, 0:CONV_WIDTH])
    gg = _dot(hb, win_ref[:, CONV_WIDTH:2 * CONV_WIDTH])
    ybuf[HALO:HALO + TM, :] = ga * _sigmoid(gg)
    cw = convw_ref[...]
    for c in range(TM // RC):
        acc = None
        for k in range(CONV_KSIZE):
            r0 = c * RC + HALO - (CONV_KSIZE - 1) + k
            term = ybuf[r0:r0 + RC, :] * cw[k:k + 1, :]
            acc = term if acc is None else acc + term
        cv = _group_layer_norm(acc + convb_ref[...], a64, clng_ref[...], clnb_ref[...])
        cat[c * RC:(c + 1) * RC, 0:CONV_WIDTH] = (cv * _sigmoid(cv)).astype(BF16)
    ybuf[0:HALO, :] = ybuf[TM:TM + HALO, :]

    off = 2 * CONV_WIDTH
    zu = _gelu_tanh(_dot(hb, win_ref[:, off:off + GMLP_WIDTH]))
    zv = _gelu_tanh(_dot(hb, win_ref[:, off + GMLP_WIDTH:off + 2 * GMLP_WIDTH]))
    vn = _group_layer_norm(zv, a64, slng_ref[...], slnb_ref[...])
    lane = lax.broadcasted_iota(jnp.int32, (CHUNK, LANES), 1)
    low = lane < HEAD_DIM
    for c in range(TM // CHUNK):
        rows = slice(c * CHUNK, (c + 1) * CHUNK)
        for j in range(GMLP_WIDTH // LANES):
            cols = slice(j * LANES, (j + 1) * LANES)
            vj = vn[rows, cols]
            rhs = jnp.concatenate([jnp.where(low, vj, 0.0).astype(BF16),
                                   jnp.where(low, 0.0, vj).astype(BF16)], axis=0)
            mixed = _dot(w2_ref[j], rhs) + bfull_ref[:, cols]
            cat[rows, CONV_WIDTH + j * LANES:CONV_WIDTH + (j + 1) * LANES] = (
                zu[rows, cols] * mixed).astype(BF16)

    x1 = x + gt1 * _dot(cat[...], wout_ref[...])
    x1_ref[0] = x1

    ms2 = jnp.mean(x1 * x1, axis=-1, keepdims=True)
    h2 = (x1 * lax.rsqrt(ms2 + EPS) * gffn_ref[...]) * (1.0 + sc2) + sh2
    hx_ref[:, 0, 0:D_MODEL] = h2
    hh = h2.astype(BF16)
    hl = (h2 - hh.astype(F32)).astype(BF16)
    logits = (_dot(hh, wrh_ref[...]) + _dot(hl, wrh_ref[...]) + _dot(hh, wrl_ref[...])
              + br_ref[...])

    lane_t = lax.broadcasted_iota(jnp.int32, (TM, LANES), 1).astype(F32)
    neg = jnp.float32(-jnp.inf)
    far = jnp.float32(LANES)
    cm = lane_t < N_GROUPS
    m = jnp.max(jnp.where(cm, logits, neg), axis=-1, keepdims=True)
    ec = jnp.exp(jnp.where(cm, logits - m, neg))
    p_grp = 1.0 / jnp.sum(ec, axis=-1, keepdims=True)
    grp = jnp.min(jnp.where(cm & (logits == m), lane_t, far), axis=-1, keepdims=True)

    lf = jnp.zeros((TM, LANES), F32)
    for g in range(N_GROUPS):
        s = N_GROUPS + g * EXPERTS_PER_GROUP
        lf = jnp.where(grp == g, pltpu.roll(logits, LANES - s, axis=1), lf)
    fm = lane_t < EXPERTS_PER_GROUP
    mf = jnp.max(jnp.where(fm, lf, neg), axis=-1, keepdims=True)
    ef = jnp.exp(jnp.where(fm, lf - mf, neg))
    fp = ef / jnp.sum(ef, axis=-1, keepdims=True)
    v1 = jnp.max(jnp.where(fm, fp, -1.0), axis=-1, keepdims=True)
    i1 = jnp.min(jnp.where(fm & (fp == v1), lane_t, far), axis=-1, keepdims=True)
    rest = fm & (lane_t != i1)
    v2 = jnp.max(jnp.where(rest, fp, -1.0), axis=-1, keepdims=True)
    i2 = jnp.min(jnp.where(rest & (fp == v2), lane_t, far), axis=-1, keepdims=True)
    den = v1 + v2
    gate = p_grp * jnp.where(lane_t == i1, v1 / den, jnp.where(lane_t == i2, v2 / den, 0.0))

    onehot = (lane_t == grp).astype(F32)
    before = _dot(ltri_ref[...], onehot.astype(BF16)) + carry[...]
    rank = jnp.sum(before * onehot, axis=-1, keepdims=True)
    carry[...] = carry[...] + jnp.sum(onehot, axis=0, keepdims=True)
    cnt_ref[...] = jnp.broadcast_to(carry[...], (8, LANES))

    route = jnp.where(lane_t == LANE_GRP, grp,
                      jnp.where(lane_t == LANE_RANK, rank, gate))
    hx_ref[:, 0, D_MODEL:ROW_W] = route


def _mixer(x, mod3, prm):
    bsz, seq, _ = x.shape
    nt = seq // TM
    const = lambda shape: pl.BlockSpec(shape, lambda b, i: (0,) * len(shape))
    in_specs = [
        pl.BlockSpec((1, TM, D_MODEL), lambda b, i: (b, i, 0)),
        pl.BlockSpec((1, 8, D_MODEL), lambda b, i: (b, 0, 0)),
        const((1, D_MODEL)),
        const((D_MODEL, 4 * CONV_WIDTH)),
        const((HALO, CONV_WIDTH)),
        const((1, CONV_WIDTH)), const((1, CONV_WIDTH)), const((1, CONV_WIDTH)),
        const((1, GMLP_WIDTH)), const((1, GMLP_WIDTH)),
        const((CONV_WIDTH, CONV_WIDTH)),
        const((GMLP_WIDTH // LANES, CHUNK, 2 * CHUNK)),
        const((CHUNK, GMLP_WIDTH)),
        const((D_MODEL, D_MODEL)),
        const((1, D_MODEL)),
        const((D_MODEL, LANES)), const((D_MODEL, LANES)), const((1, LANES)),
        const((TM, TM)),
    ]
    out_shape = (
        jax.ShapeDtypeStruct((bsz, seq, D_MODEL), F32),
        jax.ShapeDtypeStruct((bsz * seq, 1, ROW_W), F32),
        jax.ShapeDtypeStruct((8, LANES), F32),
    )
    out_specs = (
        pl.BlockSpec((1, TM, D_MODEL), lambda b, i: (b, i, 0)),
        pl.BlockSpec((TM, 1, ROW_W), lambda b, i: (b * nt + i, 0, 0)),
        pl.BlockSpec((8, LANES), lambda b, i: (0, 0)),
    )
    return pl.pallas_call(
        _mixer_kernel,
        out_shape=out_shape,
        grid=(bsz, nt),
        in_specs=in_specs,
        out_specs=out_specs,
        scratch_shapes=[
            pltpu.VMEM((TM + HALO, CONV_WIDTH), F32),
            pltpu.VMEM((TM, D_MODEL), BF16),
            pltpu.VMEM((1, LANES), F32),
        ],
        compiler_params=pltpu.CompilerParams(
            dimension_semantics=("arbitrary", "arbitrary"),
            vmem_limit_bytes=VMEM_LIMIT),
        name="mixer",
    )(x, mod3, *prm)


def _row_copy(src, dst, sem, s, d):
    return pltpu.make_async_copy(src.at[s], dst.at[d], sem)


def _dispatch_kernel(pos_ref, tail_ref, nv_ref, hx_ref, hs_ref, zbuf, zsem, sem):
    i = pl.program_id(0)

    @pl.when(i == 0)
    def _():
        zbuf[...] = jnp.zeros((TB, 1, ROW_W), F32)

        def clear(row):
            return pltpu.make_async_copy(zbuf, hs_ref.at[pl.ds(row, TB)], zsem)

        for g in range(N_GROUPS):
            clear(tail_ref[g]).start()
        for g in range(N_GROUPS):
            clear(tail_ref[g]).wait()

        def clear_block(j, carry):
            clear(j * TB).start()
            clear(j * TB).wait()
            return carry

        lax.fori_loop(nv_ref[0], hs_ref.shape[0] // TB, clear_block, 0)

    base = i * TS

    def issue(r, carry):
        t = base + r
        _row_copy(hx_ref, hs_ref, sem, t, pos_ref[t]).start()
        return carry

    lax.fori_loop(0, TS, issue, 0)

    def drain(r, carry):
        _row_copy(hx_ref, hs_ref, sem, 0, 0).wait()
        return carry

    lax.fori_loop(0, TS, drain, 0)


def _dispatch(pos, tails, n_valid, hx, n_rows):
    t = hx.shape[0]
    return pl.pallas_call(
        _dispatch_kernel,
        out_shape=jax.ShapeDtypeStruct((n_rows, 1, ROW_W), F32),
        grid_spec=pltpu.PrefetchScalarGridSpec(
            num_scalar_prefetch=3,
            grid=(t // TS,),
            in_specs=[pl.BlockSpec(memory_space=pl.ANY)],
            out_specs=pl.BlockSpec(memory_space=pl.ANY),
            scratch_shapes=[
                pltpu.VMEM((TB, 1, ROW_W), F32),
                pltpu.SemaphoreType.DMA,
                pltpu.SemaphoreType.DMA,
            ]),
        compiler_params=pltpu.CompilerParams(dimension_semantics=("arbitrary",)),
        name="dispatch",
    )(pos, tails, n_valid, hx)


def _experts_kernel(bg_ref, nv_ref, hs_ref, wg_ref, wu_ref, wd_ref, ys_ref, rows, hid):
    i = pl.program_id(0)

    @pl.when(i < nv_ref[0])
    def _():
        rows[...] = hs_ref[...].reshape(TB, ROW_W)
        xb = rows[:, 0:D_MODEL].astype(BF16)
        route = rows[:, D_MODEL:ROW_W]
        for e in range(EXPERTS_PER_GROUP):
            g = _dot(xb, wg_ref[e])
            u = _dot(xb, wu_ref[e])
            act = (g * _sigmoid(g)) * u * route[:, e:e + 1]
            hid[:, e * D_EXPERT:(e + 1) * D_EXPERT] = act.astype(BF16)
        ys_ref[...] = _dot(hid[...], wd_ref[...]).reshape(TB, 1, D_MODEL)

    @pl.when(i >= nv_ref[0])
    def _():
        ys_ref[...] = jnp.zeros((TB, 1, D_MODEL), F32)


def _experts(blk_grp, n_valid, hs, wg, wu, wd, n_blocks):
    def row_map(i, bg, nv):
        return (jnp.minimum(i, nv[0] - 1), 0, 0)

    def out_map(i, bg, nv):
        return (i, 0, 0)

    def w_map(i, bg, nv):
        return (bg[i], 0, 0)

    def wd_map(i, bg, nv):
        return (bg[i], 0)

    return pl.pallas_call(
        _experts_kernel,
        out_shape=jax.ShapeDtypeStruct((n_blocks * TB, 1, D_MODEL), F32),
        grid_spec=pltpu.PrefetchScalarGridSpec(
            num_scalar_prefetch=2,
            grid=(n_blocks,),
            in_specs=[
                pl.BlockSpec((TB, 1, ROW_W), row_map),
                pl.BlockSpec((EXPERTS_PER_GROUP, D_MODEL, D_EXPERT), w_map),
                pl.BlockSpec((EXPERTS_PER_GROUP, D_MODEL, D_EXPERT), w_map),
                pl.BlockSpec((EXPERTS_PER_GROUP * D_EXPERT, D_MODEL), wd_map),
            ],
            out_specs=pl.BlockSpec((TB, 1, D_MODEL), out_map),
            scratch_shapes=[pltpu.VMEM((TB, ROW_W), F32),
                            pltpu.VMEM((TB, EXPERTS_PER_GROUP * D_EXPERT), BF16)]),
        compiler_params=pltpu.CompilerParams(
            dimension_semantics=("arbitrary",),
            vmem_limit_bytes=VMEM_LIMIT),
        name="experts",
    )(blk_grp, n_valid, hs, wg, wu, wd)


def _combine_kernel(pos_ref, x1_ref, mod_ref, gfin_ref, ys_ref, o_ref, ybuf, y2d, sems):
    i = pl.program_id(0)
    n = pl.num_programs(0)

    def fetch(tile, slot):
        def issue(r, carry):
            p = pos_ref[tile * TC + r]
            pltpu.make_async_copy(ys_ref.at[p], ybuf.at[slot, r], sems.at[slot]).start()
            return carry
        lax.fori_loop(0, TC, issue, 0)

    @pl.when(i == 0)
    def _():
        fetch(0, 0)

    @pl.when(i + 1 < n)
    def _():
        fetch(i + 1, (i + 1) % 2)

    slot = i % 2

    def drain(r, carry):
        pltpu.make_async_copy(ys_ref.at[0], ybuf.at[slot, 0], sems.at[slot]).wait()
        return carry

    lax.fori_loop(0, TC, drain, 0)

    gt2 = mod_ref[0][5:6]
    y2d[...] = ybuf[slot].reshape(TC, D_MODEL)
    x2 = x1_ref[...] + gt2 * y2d[...]
    ms = jnp.mean(x2 * x2, axis=-1, keepdims=True)
    o_ref[...] = x2 * lax.rsqrt(ms + EPS) * gfin_ref[...]


def _combine(pos, x1, mod3, g_final, ys, seq):
    t = x1.shape[0]
    per_batch = seq // TC
    return pl.pallas_call(
        _combine_kernel,
        out_shape=jax.ShapeDtypeStruct((t, D_MODEL), F32),
        grid_spec=pltpu.PrefetchScalarGridSpec(
            num_scalar_prefetch=1,
            grid=(t // TC,),
            in_specs=[
                pl.BlockSpec((TC, D_MODEL), lambda i, pos: (i, 0)),
                pl.BlockSpec((1, 8, D_MODEL), lambda i, pos: (i // per_batch, 0, 0)),
                pl.BlockSpec((1, D_MODEL), lambda i, pos: (0, 0)),
                pl.BlockSpec(memory_space=pl.ANY),
            ],
            out_specs=pl.BlockSpec((TC, D_MODEL), lambda i, pos: (i, 0)),
            scratch_shapes=[
                pltpu.VMEM((2, TC, 1, D_MODEL), F32),
                pltpu.VMEM((TC, D_MODEL), F32),
                pltpu.SemaphoreType.DMA((2,)),
            ]),
        compiler_params=pltpu.CompilerParams(dimension_semantics=("arbitrary",)),
        name="combine",
    )(pos, x1, mod3, g_final, ys)


def _split_bf16(w):
    hi = w.astype(BF16)
    return hi, (w - hi.astype(F32)).astype(BF16)


def kernel(x, c, w_ada, b_ada, g_mix, w_in, conv_w, conv_b, conv_ln_g, conv_ln_b, sgu_ln_g,
           sgu_ln_b, sgu_w, sgu_b, w_out, g_ffn, w_coarse, b_coarse, w_fine, b_fine, w_gate,
           w_up, w_down, g_final):
    bsz, seq, d = x.shape
    t = bsz * seq
    n_blocks = t // TB + N_GROUPS
    n_rows = (n_blocks + 1) * TB

    c_pad = jnp.zeros((8, d), F32).at[:bsz].set(c)
    mod = _ada(c_pad, w_ada[0], b_ada[0][None, :])
    mod3 = jnp.pad(mod[:bsz].reshape(bsz, 6, d), ((0, 0), (0, 2), (0, 0)))

    tril = jnp.tril(jnp.ones((CHUNK, CHUNK), F32))
    ws = (sgu_w[0] * tril).astype(BF16)
    w2 = jnp.concatenate([ws[0::2], ws[1::2]], axis=2)
    bfull = jnp.repeat(sgu_b[0].T, HEAD_DIM, axis=1)
    ch = jnp.arange(CONV_WIDTH) // HEAD_DIM
    a64 = ((ch[:, None] == ch[None, :]).astype(F32) / HEAD_DIM).astype(BF16)
    convw = jnp.pad(conv_w[0], ((0, HALO - CONV_KSIZE), (0, 0)))
    wr = jnp.zeros((d, LANES), F32)
    wr = wr.at[:, :N_GROUPS].set(w_coarse[0])
    wr = wr.at[:, N_GROUPS:N_GROUPS * (1 + EXPERTS_PER_GROUP)].set(
        jnp.transpose(w_fine[0], (1, 0, 2)).reshape(d, N_GROUPS * EXPERTS_PER_GROUP))
    br = jnp.zeros((1, LANES), F32)
    br = br.at[0, :N_GROUPS].set(b_coarse[0])
    br = br.at[0, N_GROUPS:N_GROUPS * (1 + EXPERTS_PER_GROUP)].set(b_fine[0].reshape(-1))
    wrh, wrl = _split_bf16(wr)
    ri = jnp.arange(TM)
    ltri = (ri[:, None] > ri[None, :]).astype(BF16)
    prm = (g_mix, w_in[0].astype(BF16), convw, conv_b, conv_ln_g[0].reshape(1, -1),
           conv_ln_b[0].reshape(1, -1), sgu_ln_g[0].reshape(1, -1), sgu_ln_b[0].reshape(1, -1),
           a64, w2, bfull, w_out[0].astype(BF16), g_ffn, wrh, wrl, br, ltri)

    x1, hx, cnt = _mixer(x, mod3, prm)

    counts = cnt[0, :N_GROUPS].astype(jnp.int32)
    nblk = (counts + TB - 1) // TB
    blk_end = jnp.cumsum(nblk)
    blk_start = blk_end - nblk
    grp = hx[:, 0, D_MODEL + LANE_GRP].astype(jnp.int32)
    rank = hx[:, 0, D_MODEL + LANE_RANK].astype(jnp.int32)
    pos = blk_start[grp] * TB + rank
    tails = blk_start * TB + counts
    n_valid = blk_end[N_GROUPS - 1:]
    blk_ids = jnp.minimum(jnp.arange(n_blocks, dtype=jnp.int32), n_valid[0] - 1)
    blk_grp = jnp.sum((blk_ids[:, None] >= blk_end[None, :]).astype(jnp.int32), axis=1)

    hs = _dispatch(pos, tails, n_valid, hx, n_rows)
    ys = _experts(blk_grp, n_valid, hs,
                  w_gate[0].astype(BF16), w_up[0].astype(BF16),
                  w_down[0].astype(BF16).reshape(N_GROUPS * EXPERTS_PER_GROUP * D_EXPERT, d),
                  n_blocks)
    out = _combine(pos, x1.reshape(t, d), mod3, g_final[None, :], ys, seq)
    return out.reshape(bsz, seq, d)
```

```python
import jax
import jax.numpy as jnp
from jax import lax
from jax.experimental import pallas as pl
from jax.experimental.pallas import tpu as pltpu

F32 = jnp.float32
BF16 = jnp.bfloat16

D_MODEL = 1024
CONV_WIDTH = 512
CONV_KSIZE = 31
GMLP_WIDTH = 512
HEAD_DIM = 64
CHUNK = 128
N_GROUPS = 4
EXPERTS_PER_GROUP = 8
D_EXPERT = 256
EPS = 1e-6
assert EXPERTS_PER_GROUP == 8

LANES = 128
SUB = 8
TM = 512
HALO = 32
TB = 256
TS = 512
TC = 256
ACT_ROWS = D_MODEL // LANES
ROUTE_ROW = ACT_ROWS
TOK_ROWS = 2 * SUB
META_GRP = 0
META_RANK = 1
ROUTER_ROWS = SUB * (1 + N_GROUPS) + SUB
DMA_UNROLL = 8
VMEM_LIMIT = 56 * 1024 * 1024


def _dot(a, b):
    return jnp.dot(a, b, preferred_element_type=F32)


def _sigmoid(x):
    return 0.5 * jnp.tanh(0.5 * x) + 0.5


def _gelu_tanh(x):
    c = 0.7978845608028654
    return 0.5 * x * (1.0 + jnp.tanh(c * (x + 0.044715 * (x * x * x))))


def _tile_rows(c, n, per_token):
    return pl.ds(c, n, stride=per_token)


def _token_tile(t, per_token):
    return pl.ds(pl.multiple_of(t * per_token, per_token), per_token)


def _for_each_token(n, fn):
    def group(gi, carry):
        for j in range(DMA_UNROLL):
            fn(gi * DMA_UNROLL + j, j)
        return carry
    lax.fori_loop(0, n // DMA_UNROLL, group, 0)


def _ada_kernel(c_ref, w_ref, b_ref, o_ref):
    c = c_ref[...]
    ca = c * _sigmoid(c)
    o_ref[...] = jnp.dot(ca, w_ref[...], preferred_element_type=F32,
                         precision=lax.Precision.HIGHEST) + b_ref[...]


def _ada(c_pad, w_ada, b_ada):
    n = w_ada.shape[1]
    tn = 1024
    return pl.pallas_call(
        _ada_kernel,
        out_shape=jax.ShapeDtypeStruct((SUB, n), F32),
        grid=(n // tn,),
        in_specs=[
            pl.BlockSpec((SUB, D_MODEL), lambda j: (0, 0)),
            pl.BlockSpec((D_MODEL, tn), lambda j: (0, j)),
            pl.BlockSpec((1, tn), lambda j: (0, j)),
        ],
        out_specs=pl.BlockSpec((SUB, tn), lambda j: (0, j)),
        name="ada",
    )(c_pad, w_ada, b_ada)


def _group_layer_norm(v, a64, g, b):
    mu = _dot(v.astype(BF16), a64)
    d = v - mu
    var = _dot((d * d).astype(BF16), a64)
    return d * lax.rsqrt(var + EPS) * g + b


def _causal_conv_chunk(ybuf, cw, r0):
    out = None
    for r in range(SUB):
        part = None
        for q in range((CONV_KSIZE - 1 - r) // SUB + 1):
            k = CONV_KSIZE - 1 - (SUB * q + r)
            seg = ybuf[pl.ds(r0 + HALO - SUB - SUB * q, CHUNK + SUB), :]
            term = seg * cw[k:k + 1, :]
            part = term if part is None else part + term
        shifted = part[SUB - r:SUB - r + CHUNK, :]
        out = shifted if out is None else out + shifted
    return out


def _mixer_kernel(x_ref, mod_ref, gmix_ref, win_ref, convw_ref, convb_ref, clng_ref, clnb_ref,
                  slng_ref, slnb_ref, a64_ref, w2_ref, bfull_ref, wout_ref, gffn_ref,
                  wrh_ref, wrl_ref, br_ref, utri_ref,
                  x1_ref, hx_ref, meta_ref, cnt_ref,
                  ybuf, zu_s, vn_s, cat, carry):
    b = pl.program_id(0)
    i = pl.program_id(1)

    @pl.when(i == 0)
    def _():
        ybuf[0:HALO, :] = jnp.zeros((HALO, CONV_WIDTH), F32)

    @pl.when((b == 0) & (i == 0))
    def _():
        carry[...] = jnp.zeros((SUB, 1), F32)

    x = x_ref[0]
    mod = mod_ref[0]
    sh1, sc1, gt1 = mod[0:1], mod[1:2], mod[2:3]
    sh2, sc2 = mod[3:4], mod[4:5]

    ms = jnp.mean(x * x, axis=-1, keepdims=True)
    h = x * lax.rsqrt(ms + EPS) * (gmix_ref[...] * (1.0 + sc1)) + sh1
    hb = h.astype(BF16)
    a64 = a64_ref[...]

    ga = _dot(hb, win_ref[:, 0:CONV_WIDTH])
    gg = _dot(hb, win_ref[:, CONV_WIDTH:2 * CONV_WIDTH])
    ybuf[HALO:HALO + TM, :] = ga * _sigmoid(gg)
    off = 2 * CONV_WIDTH
    zu_s[...] = _gelu_tanh(_dot(hb, win_ref[:, off:off + GMLP_WIDTH]))
    zv = _gelu_tanh(_dot(hb, win_ref[:, off + GMLP_WIDTH:off + 2 * GMLP_WIDTH]))
    vn_s[...] = _group_layer_norm(zv, a64, slng_ref[...], slnb_ref[...])

    cw = convw_ref[...]
    low = lax.broadcasted_iota(jnp.int32, (CHUNK, LANES), 1) < HEAD_DIM

    for c in range(TM // CHUNK):
        r0 = c * CHUNK
        rows = pl.ds(r0, CHUNK)
        cv = _causal_conv_chunk(ybuf, cw, r0) + convb_ref[...]
        cv = _group_layer_norm(cv, a64, clng_ref[...], clnb_ref[...])
        cat[rows, 0:CONV_WIDTH] = (cv * _sigmoid(cv)).astype(BF16)
        for j in range(GMLP_WIDTH // LANES):
            cols = slice(j * LANES, (j + 1) * LANES)
            vj = vn_s[rows, cols]
            rhs = jnp.concatenate([jnp.where(low, vj, 0.0).astype(BF16),
                                   jnp.where(low, 0.0, vj).astype(BF16)], axis=0)
            mixed = _dot(w2_ref[j], rhs) + bfull_ref[:, cols]
            cat[rows, CONV_WIDTH + j * LANES:CONV_WIDTH + (j + 1) * LANES] = (
                zu_s[rows, cols] * mixed).astype(BF16)
    ybuf[0:HALO, :] = ybuf[TM:TM + HALO, :]

    x1 = x + gt1 * _dot(cat[...], wout_ref[...])
    x1_ref[0] = x1

    ms2 = jnp.mean(x1 * x1, axis=-1, keepdims=True)
    h2 = x1 * lax.rsqrt(ms2 + EPS) * (gffn_ref[...] * (1.0 + sc2)) + sh2
    hh = h2.astype(BF16)
    hl = (h2 - hh.astype(F32)).astype(BF16)

    def nt_dot(w, a):
        return lax.dot_general(w, a, (((1,), (1,)), ((), ())), preferred_element_type=F32)

    lt = (nt_dot(wrh_ref[...], hh) + nt_dot(wrh_ref[...], hl) + nt_dot(wrl_ref[...], hh)
          + br_ref[...])
    row = lax.broadcasted_iota(jnp.int32, (SUB, TM), 0).astype(F32)
    neg = jnp.float32(-jnp.inf)
    far = jnp.float32(SUB)
    cm = row < N_GROUPS
    lc = lt[0:SUB]
    m = jnp.max(jnp.where(cm, lc, neg), axis=0, keepdims=True)
    ec = jnp.exp(jnp.where(cm, lc - m, neg))
    p_grp = 1.0 / jnp.sum(ec, axis=0, keepdims=True)
    grp = jnp.min(jnp.where(cm & (lc == m), row, far), axis=0, keepdims=True)

    lf = lt[SUB:2 * SUB]
    for g in range(1, N_GROUPS):
        lf = jnp.where(grp == g, lt[(g + 1) * SUB:(g + 2) * SUB], lf)
    mf = jnp.max(lf, axis=0, keepdims=True)
    ef = jnp.exp(lf - mf)
    fp = ef / jnp.sum(ef, axis=0, keepdims=True)
    v1 = jnp.max(fp, axis=0, keepdims=True)
    i1 = jnp.min(jnp.where(fp == v1, row, far), axis=0, keepdims=True)
    rest = row != i1
    v2 = jnp.max(jnp.where(rest, fp, -1.0), axis=0, keepdims=True)
    i2 = jnp.min(jnp.where(rest & (fp == v2), row, far), axis=0, keepdims=True)
    den = v1 + v2
    gate = p_grp * jnp.where(row == i1, v1 / den, jnp.where(row == i2, v2 / den, 0.0))

    onehot = (row == grp).astype(F32)
    before = _dot(onehot.astype(BF16), utri_ref[...]) + carry[...]
    rank = jnp.sum(before * onehot, axis=0, keepdims=True)
    carry[...] = carry[...] + jnp.sum(onehot, axis=1, keepdims=True)
    cnt_ref[...] = jnp.broadcast_to(carry[...], (SUB, LANES))
    meta_ref[...] = jnp.where(row == META_GRP, grp, jnp.where(row == META_RANK, rank, 0.0))

    route = jnp.concatenate([gate, jnp.zeros((LANES - SUB, TM), F32)], axis=0).T

    hx_ref[...] = jnp.zeros((TM * TOK_ROWS, LANES), F32)
    for c in range(ACT_ROWS):
        hx_ref[_tile_rows(c, TM, TOK_ROWS), :] = h2[:, c * LANES:(c + 1) * LANES]
    hx_ref[_tile_rows(ROUTE_ROW, TM, TOK_ROWS), :] = route


def _mixer(x, mod3, prm):
    bsz, seq, _ = x.shape
    nt = seq // TM
    const = lambda shape: pl.BlockSpec(shape, lambda b, i: (0,) * len(shape))
    in_specs = [
        pl.BlockSpec((1, TM, D_MODEL), lambda b, i: (b, i, 0)),
        pl.BlockSpec((1, SUB, D_MODEL), lambda b, i: (b, 0, 0)),
        const((1, D_MODEL)),
        const((D_MODEL, 4 * CONV_WIDTH)),
        const((HALO, CONV_WIDTH)),
        const((1, CONV_WIDTH)), const((1, CONV_WIDTH)), const((1, CONV_WIDTH)),
        const((1, GMLP_WIDTH)), const((1, GMLP_WIDTH)),
        const((CONV_WIDTH, CONV_WIDTH)),
        const((GMLP_WIDTH // LANES, CHUNK, 2 * CHUNK)),
        const((CHUNK, GMLP_WIDTH)),
        const((D_MODEL, D_MODEL)),
        const((1, D_MODEL)),
        const((ROUTER_ROWS, D_MODEL)), const((ROUTER_ROWS, D_MODEL)), const((ROUTER_ROWS, 1)),
        const((TM, TM)),
    ]
    out_shape = (
        jax.ShapeDtypeStruct((bsz, seq, D_MODEL), F32),
        jax.ShapeDtypeStruct((bsz * seq * TOK_ROWS, LANES), F32),
        jax.ShapeDtypeStruct((SUB, bsz * seq), F32),
        jax.ShapeDtypeStruct((SUB, LANES), F32),
    )
    out_specs = (
        pl.BlockSpec((1, TM, D_MODEL), lambda b, i: (b, i, 0)),
        pl.BlockSpec((TM * TOK_ROWS, LANES), lambda b, i: (b * nt + i, 0)),
        pl.BlockSpec((SUB, TM), lambda b, i: (0, b * nt + i)),
        pl.BlockSpec((SUB, LANES), lambda b, i: (0, 0)),
    )
    return pl.pallas_call(
        _mixer_kernel,
        out_shape=out_shape,
        grid=(bsz, nt),
        in_specs=in_specs,
        out_specs=out_specs,
        scratch_shapes=[
            pltpu.VMEM((TM + HALO, CONV_WIDTH), F32),
            pltpu.VMEM((TM, GMLP_WIDTH), F32),
            pltpu.VMEM((TM, GMLP_WIDTH), F32),
            pltpu.VMEM((TM, D_MODEL), BF16),
            pltpu.VMEM((SUB, 1), F32),
        ],
        compiler_params=pltpu.CompilerParams(
            dimension_semantics=("arbitrary", "arbitrary"),
            vmem_limit_bytes=VMEM_LIMIT),
        name="mixer",
    )(x, mod3, *prm)


def _dispatch_kernel(pos_ref, tail_ref, nv_ref, hx_ref, hs_ref, zbuf, zsem, sem):
    i = pl.program_id(0)

    @pl.when(i == 0)
    def _():
        zbuf[...] = jnp.zeros((TB * TOK_ROWS, LANES), F32)

        def clear(token):
            dst = pl.ds(pl.multiple_of(token * TOK_ROWS, TOK_ROWS), TB * TOK_ROWS)
            return pltpu.make_async_copy(zbuf, hs_ref.at[dst, :], zsem)

        for g in range(N_GROUPS):
            clear(tail_ref[g]).start()
        for g in range(N_GROUPS):
            clear(tail_ref[g]).wait()

        def clear_block(j, carry):
            clear(j * TB).start()
            clear(j * TB).wait()
            return carry

        lax.fori_loop(nv_ref[0], hs_ref.shape[0] // (TB * TOK_ROWS), clear_block, 0)

    def row_copy(r, p):
        return pltpu.make_async_copy(hx_ref.at[_token_tile(r, TOK_ROWS), :],
                                     hs_ref.at[_token_tile(p, TOK_ROWS), :], sem)

    _for_each_token(TS, lambda r, j: row_copy(r, pos_ref[i * TS + r]).start(priority=j % 2))
    _for_each_token(TS, lambda r, j: row_copy(0, 0).wait())


def _dispatch(pos, tails, n_valid, hx, n_rows):
    t = hx.shape[0] // TOK_ROWS
    return pl.pallas_call(
        _dispatch_kernel,
        out_shape=jax.ShapeDtypeStruct((n_rows * TOK_ROWS, LANES), F32),
        grid_spec=pltpu.PrefetchScalarGridSpec(
            num_scalar_prefetch=3,
            grid=(t // TS,),
            in_specs=[pl.BlockSpec((TS * TOK_ROWS, LANES), lambda i, *_: (i, 0))],
            out_specs=pl.BlockSpec(memory_space=pl.ANY),
            scratch_shapes=[
                pltpu.VMEM((TB * TOK_ROWS, LANES), F32),
                pltpu.SemaphoreType.DMA,
                pltpu.SemaphoreType.DMA,
            ]),
        compiler_params=pltpu.CompilerParams(dimension_semantics=("arbitrary",)),
        name="dispatch",
    )(pos, tails, n_valid, hx)


def _experts_kernel(bg_ref, nv_ref, hs_ref, wg_ref, wu_ref, wd_ref, ys_ref, xb, hid):
    i = pl.program_id(0)

    @pl.when(i < nv_ref[0])
    def _():
        for c in range(ACT_ROWS):
            xb[:, c * LANES:(c + 1) * LANES] = hs_ref[_tile_rows(c, TB, TOK_ROWS), :].astype(BF16)
        route = hs_ref[_tile_rows(ROUTE_ROW, TB, TOK_ROWS), :]
        x = xb[...]
        for e in range(EXPERTS_PER_GROUP):
            g = _dot(x, wg_ref[e])
            u = _dot(x, wu_ref[e])
            act = (g * _sigmoid(g)) * u * route[:, e:e + 1]
            hid[:, e * D_EXPERT:(e + 1) * D_EXPERT] = act.astype(BF16)
        y = _dot(hid[...], wd_ref[...])
        for c in range(SUB):
            ys_ref[_tile_rows(c, TB, SUB), :] = y[:, c * LANES:(c + 1) * LANES]

    @pl.when(i >= nv_ref[0])
    def _():
        ys_ref[...] = jnp.zeros((TB * SUB, LANES), F32)


def _experts(blk_grp, n_valid, hs, wg, wu, wd, n_blocks):
    def row_map(i, bg, nv):
        return (jnp.minimum(i, nv[0] - 1), 0)

    def out_map(i, bg, nv):
        return (i, 0)

    def w_map(i, bg, nv):
        return (bg[i], 0, 0)

    def wd_map(i, bg, nv):
        return (bg[i], 0)

    return pl.pallas_call(
        _experts_kernel,
        out_shape=jax.ShapeDtypeStruct((n_blocks * TB * SUB, LANES), F32),
        grid_spec=pltpu.PrefetchScalarGridSpec(
            num_scalar_prefetch=2,
            grid=(n_blocks,),
            in_specs=[
                pl.BlockSpec((TB * TOK_ROWS, LANES), row_map),
                pl.BlockSpec((EXPERTS_PER_GROUP, D_MODEL, D_EXPERT), w_map),
                pl.BlockSpec((EXPERTS_PER_GROUP, D_MODEL, D_EXPERT), w_map),
                pl.BlockSpec((EXPERTS_PER_GROUP * D_EXPERT, D_MODEL), wd_map),
            ],
            out_specs=pl.BlockSpec((TB * SUB, LANES), out_map),
            scratch_shapes=[pltpu.VMEM((TB, D_MODEL), BF16),
                            pltpu.VMEM((TB, EXPERTS_PER_GROUP * D_EXPERT), BF16)]),
        compiler_params=pltpu.CompilerParams(
            dimension_semantics=("arbitrary",),
            vmem_limit_bytes=VMEM_LIMIT),
        name="experts",
    )(blk_grp, n_valid, hs, wg, wu, wd)


def _combine_kernel(pos_ref, x1_ref, mod_ref, gfin_ref, ys_ref, o_ref, ybuf, sems):
    i = pl.program_id(0)
    n = pl.num_programs(0)

    def row_copy(slot, r, p):
        return pltpu.make_async_copy(ys_ref.at[_token_tile(p, SUB), :],
                                     ybuf.at[slot, _token_tile(r, SUB), :], sems.at[slot])

    def fetch(tile, slot):
        _for_each_token(
            TC, lambda r, j: row_copy(slot, r, pos_ref[tile * TC + r]).start(priority=j % 2))

    @pl.when(i == 0)
    def _():
        fetch(0, 0)

    @pl.when(i + 1 < n)
    def _():
        fetch(i + 1, (i + 1) % 2)

    slot = i % 2

    _for_each_token(TC, lambda r, j: row_copy(slot, 0, 0).wait())

    gt2 = mod_ref[0][5:6]
    ycur = ybuf.at[slot]
    x2 = []
    ssq = jnp.zeros((TC, 1), F32)
    for c in range(SUB):
        cols = slice(c * LANES, (c + 1) * LANES)
        v = x1_ref[:, cols] + gt2[:, cols] * ycur[_tile_rows(c, TC, SUB), :]
        ssq = ssq + jnp.sum(v * v, axis=-1, keepdims=True)
        x2.append(v)
    scale = lax.rsqrt(ssq / D_MODEL + EPS)
    for c in range(SUB):
        cols = slice(c * LANES, (c + 1) * LANES)
        o_ref[:, cols] = x2[c] * scale * gfin_ref[:, cols]


def _combine(pos, x1, mod3, g_final, ys, seq):
    t = x1.shape[0]
    per_batch = seq // TC
    return pl.pallas_call(
        _combine_kernel,
        out_shape=jax.ShapeDtypeStruct((t, D_MODEL), F32),
        grid_spec=pltpu.PrefetchScalarGridSpec(
            num_scalar_prefetch=1,
            grid=(t // TC,),
            in_specs=[
                pl.BlockSpec((TC, D_MODEL), lambda i, pos: (i, 0)),
                pl.BlockSpec((1, SUB, D_MODEL), lambda i, pos: (i // per_batch, 0, 0)),
                pl.BlockSpec((1, D_MODEL), lambda i, pos: (0, 0)),
                pl.BlockSpec(memory_space=pl.ANY),
            ],
            out_specs=pl.BlockSpec((TC, D_MODEL), lambda i, pos: (i, 0)),
            scratch_shapes=[
                pltpu.VMEM((2, TC * SUB, LANES), F32),
                pltpu.SemaphoreType.DMA((2,)),
            ]),
        compiler_params=pltpu.CompilerParams(dimension_semantics=("arbitrary",)),
        name="combine",
    )(pos, x1, mod3, g_final, ys)


def _split_bf16(w):
    hi = w.astype(BF16)
    return hi, (w - hi.astype(F32)).astype(BF16)


def kernel(x, c, w_ada, b_ada, g_mix, w_in, conv_w, conv_b, conv_ln_g, conv_ln_b, sgu_ln_g,
           sgu_ln_b, sgu_w, sgu_b, w_out, g_ffn, w_coarse, b_coarse, w_fine, b_fine, w_gate,
           w_up, w_down, g_final):
    bsz, seq, d = x.shape
    t = bsz * seq
    n_blocks = t // TB + N_GROUPS
    n_rows = (n_blocks + 1) * TB

    c_pad = jnp.zeros((SUB, d), F32).at[:bsz].set(c)
    mod = _ada(c_pad, w_ada[0], b_ada[0][None, :])
    mod3 = jnp.pad(mod[:bsz].reshape(bsz, 6, d), ((0, 0), (0, 2), (0, 0)))

    tril = jnp.tril(jnp.ones((CHUNK, CHUNK), F32))
    ws = (sgu_w[0] * tril).astype(BF16)
    w2 = jnp.concatenate([ws[0::2], ws[1::2]], axis=2)
    bfull = jnp.repeat(sgu_b[0].T, HEAD_DIM, axis=1)
    ch = jnp.arange(CONV_WIDTH) // HEAD_DIM
    a64 = ((ch[:, None] == ch[None, :]).astype(F32) / HEAD_DIM).astype(BF16)
    convw = jnp.pad(conv_w[0], ((0, HALO - CONV_KSIZE), (0, 0)))
    wr = jnp.zeros((ROUTER_ROWS, d), F32)
    wr = wr.at[:N_GROUPS].set(w_coarse[0].T)
    wr = wr.at[SUB:SUB * (1 + N_GROUPS)].set(
        jnp.transpose(w_fine[0], (0, 2, 1)).reshape(N_GROUPS * EXPERTS_PER_GROUP, d))
    br = jnp.zeros((ROUTER_ROWS, 1), F32)
    br = br.at[:N_GROUPS, 0].set(b_coarse[0])
    br = br.at[SUB:SUB * (1 + N_GROUPS), 0].set(b_fine[0].reshape(-1))
    wrh, wrl = _split_bf16(wr)
    ri = jnp.arange(TM)
    utri = (ri[:, None] < ri[None, :]).astype(BF16)
    prm = (g_mix, w_in[0].astype(BF16), convw, conv_b, conv_ln_g[0].reshape(1, -1),
           conv_ln_b[0].reshape(1, -1), sgu_ln_g[0].reshape(1, -1), sgu_ln_b[0].reshape(1, -1),
           a64, w2, bfull, w_out[0].astype(BF16), g_ffn, wrh, wrl, br, utri)

    x1, hx, meta, cnt = _mixer(x, mod3, prm)

    counts = cnt[:N_GROUPS, 0].astype(jnp.int32)
    nblk = (counts + TB - 1) // TB
    blk_end = jnp.cumsum(nblk)
    blk_start = blk_end - nblk
    grp = meta[META_GRP].astype(jnp.int32)
    rank = meta[META_RANK].astype(jnp.int32)
    pos = blk_start[grp] * TB + rank
    tails = blk_start * TB + counts
    n_valid = blk_end[N_GROUPS - 1:]
    blk_ids = jnp.minimum(jnp.arange(n_blocks, dtype=jnp.int32), n_valid[0] - 1)
    blk_grp = jnp.sum((blk_ids[:, None] >= blk_end[None, :]).astype(jnp.int32), axis=1)

    hs = _dispatch(pos, tails, n_valid, hx, n_rows)
    ys = _experts(blk_grp, n_valid, hs,
                  w_gate[0].astype(BF16), w_up[0].astype(BF16),
                  w_down[0].astype(BF16).reshape(N_GROUPS * EXPERTS_PER_GROUP * D_EXPERT, d),
                  n_blocks)
    out = _combine(pos, x1.reshape(t, d), mod3, g_final[None, :], ys, seq)
    return out.reshape(bsz, seq, d)
```

```python
import functools

import jax
import jax.numpy as jnp
from jax import lax
from jax.experimental import pallas as pl
from jax.experimental.pallas import tpu as pltpu

F32 = jnp.float32
BF16 = jnp.bfloat16

D_MODEL = 1024
CONV_WIDTH = 512
CONV_KSIZE = 31
GMLP_WIDTH = 512
HEAD_DIM = 64
CHUNK = 128
N_GROUPS = 4
EXPERTS_PER_GROUP = 8
D_EXPERT = 256
EPS = 1e-6
assert EXPERTS_PER_GROUP == 8

LANES = 128
SUB = 8
TM = 512
HALO = 32
TB = 256
TS = 512
TC = 256
ACT_ROWS = D_MODEL // LANES
ROUTE_ROW = ACT_ROWS
TOK_ROWS = 2 * SUB
META_GRP = 0
META_RANK = 1
ROUTER_ROWS = SUB * (1 + N_GROUPS) + SUB
DMA_UNROLL = 8
VMEM_LIMIT = 56 * 1024 * 1024


def _dot(a, b):
    return jnp.dot(a, b, preferred_element_type=F32)


def _sigmoid(x):
    return 0.5 * jnp.tanh(0.5 * x) + 0.5


def _gelu_tanh(x):
    c = 0.7978845608028654
    return 0.5 * x * (1.0 + jnp.tanh(c * (x + 0.044715 * (x * x * x))))


def _tile_rows(c, n, per_token):
    return pl.ds(c, n, stride=per_token)


def _token_tile(t, per_token):
    return pl.ds(pl.multiple_of(t * per_token, per_token), per_token)


def _for_each_token(n, fn):
    def group(gi, carry):
        for j in range(DMA_UNROLL):
            fn(gi * DMA_UNROLL + j, j)
        return carry
    lax.fori_loop(0, n // DMA_UNROLL, group, 0)


def _ada_kernel(ct_ref, w_ref, b_ref, o_ref):
    ct = ct_ref[...]
    ca = ct * _sigmoid(ct)
    w = w_ref[...]
    for b in range(ct.shape[1]):
        o_ref[b:b + 1, :] = jnp.sum(w * ca[:, b:b + 1], axis=0, keepdims=True) + b_ref[...]


def _ada(c_t, w_ada, b_ada):
    n = w_ada.shape[1]
    bsz = c_t.shape[1]
    tn = 1024
    return pl.pallas_call(
        _ada_kernel,
        out_shape=jax.ShapeDtypeStruct((bsz, n), F32),
        grid=(n // tn,),
        in_specs=[
            pl.BlockSpec((D_MODEL, bsz), lambda j: (0, 0)),
            pl.BlockSpec((D_MODEL, tn), lambda j: (0, j)),
            pl.BlockSpec((1, tn), lambda j: (0, j)),
        ],
        out_specs=pl.BlockSpec((bsz, tn), lambda j: (0, j)),
        name="ada",
    )(c_t, w_ada, b_ada)


def _group_layer_norm(v, a64, g, b):
    mu = _dot(v.astype(BF16), a64)
    d = v - mu
    var = _dot((d * d).astype(BF16), a64)
    return d * lax.rsqrt(var + EPS) * g + b


def _causal_conv_chunk(ybuf, cw, r0):
    out = None
    for r in range(SUB):
        part = None
        for q in range((CONV_KSIZE - 1 - r) // SUB + 1):
            k = CONV_KSIZE - 1 - (SUB * q + r)
            seg = ybuf[pl.ds(r0 + HALO - SUB - SUB * q, CHUNK + SUB), :]
            term = seg * cw[k:k + 1, :]
            part = term if part is None else part + term
        shifted = part[SUB - r:SUB - r + CHUNK, :]
        out = shifted if out is None else out + shifted
    return out


def _mixer_kernel(x_ref, xp_ref, mod_ref, modp_ref, gmix_ref, win_ref, convw_ref, convb_ref,
                  clng_ref, clnb_ref, slng_ref, slnb_ref, a64_ref, w2_ref, bfull_ref, wout_ref,
                  gffn_ref, wrh_ref, wrl_ref, br_ref, utri_ref,
                  x1_ref, hx_ref, meta_ref, cnt_ref,
                  ybuf, zu_s, vn_s, cat_w, cat_r, h2_s, route_s, carry, *, tiles_per_seq):
    s = pl.program_id(0)
    n_tiles = pl.num_programs(0) - 1

    @pl.when(lax.rem(jnp.minimum(s, n_tiles - 1), tiles_per_seq) == 0)
    def _():
        ybuf[0:HALO, :] = jnp.zeros((HALO, CONV_WIDTH), F32)

    @pl.when(s == 0)
    def _():
        carry[...] = jnp.zeros((SUB, 1), F32)
        cat_r[...] = jnp.zeros((TM, D_MODEL), BF16)

    counted = jnp.where(s > 0, 1.0, 0.0)

    x = x_ref[0]
    mod = mod_ref[0]
    sh1, sc1 = mod[0:1], mod[1:2]

    ms = jnp.mean(x * x, axis=-1, keepdims=True)
    h = x * lax.rsqrt(ms + EPS) * (gmix_ref[...] * (1.0 + sc1)) + sh1
    hb = h.astype(BF16)
    a64 = a64_ref[...]

    ga = _dot(hb, win_ref[:, 0:CONV_WIDTH])
    gg = _dot(hb, win_ref[:, CONV_WIDTH:2 * CONV_WIDTH])
    ybuf[HALO:HALO + TM, :] = ga * _sigmoid(gg)
    off = 2 * CONV_WIDTH
    zu_s[...] = _gelu_tanh(_dot(hb, win_ref[:, off:off + GMLP_WIDTH]))
    zv = _gelu_tanh(_dot(hb, win_ref[:, off + GMLP_WIDTH:off + 2 * GMLP_WIDTH]))
    vn_s[...] = _group_layer_norm(zv, a64, slng_ref[...], slnb_ref[...])

    cw = convw_ref[...]
    low = lax.broadcasted_iota(jnp.int32, (CHUNK, LANES), 1) < HEAD_DIM

    def back_residual():
        gt1 = modp_ref[0][2:3]
        x1_ref[0] = xp_ref[0] + gt1 * _dot(cat_r[...], wout_ref[...])

    def back_router():
        modp = modp_ref[0]
        sh2, sc2 = modp[3:4], modp[4:5]
        x1 = x1_ref[0]
        ms2 = jnp.mean(x1 * x1, axis=-1, keepdims=True)
        h2 = x1 * lax.rsqrt(ms2 + EPS) * (gffn_ref[...] * (1.0 + sc2)) + sh2
        h2_s[...] = h2
        hh = h2.astype(BF16)
        hl = (h2 - hh.astype(F32)).astype(BF16)

        def nt_dot(w, a):
            return lax.dot_general(w, a, (((1,), (1,)), ((), ())), preferred_element_type=F32)

        lt = (nt_dot(wrh_ref[...], hh) + nt_dot(wrh_ref[...], hl) + nt_dot(wrl_ref[...], hh)
              + br_ref[...])
        row = lax.broadcasted_iota(jnp.int32, (SUB, TM), 0).astype(F32)
        neg = jnp.float32(-jnp.inf)
        far = jnp.float32(SUB)
        cm = row < N_GROUPS
        lc = lt[0:SUB]
        m = jnp.max(jnp.where(cm, lc, neg), axis=0, keepdims=True)
        ec = jnp.exp(jnp.where(cm, lc - m, neg))
        p_grp = 1.0 / jnp.sum(ec, axis=0, keepdims=True)
        grp = jnp.min(jnp.where(cm & (lc == m), row, far), axis=0, keepdims=True)

        lf = lt[SUB:2 * SUB]
        for g in range(1, N_GROUPS):
            lf = jnp.where(grp == g, lt[(g + 1) * SUB:(g + 2) * SUB], lf)
        mf = jnp.max(lf, axis=0, keepdims=True)
        ef = jnp.exp(lf - mf)
        fp = ef / jnp.sum(ef, axis=0, keepdims=True)
        v1 = jnp.max(fp, axis=0, keepdims=True)
        i1 = jnp.min(jnp.where(fp == v1, row, far), axis=0, keepdims=True)
        rest = row != i1
        v2 = jnp.max(jnp.where(rest, fp, -1.0), axis=0, keepdims=True)
        i2 = jnp.min(jnp.where(rest & (fp == v2), row, far), axis=0, keepdims=True)
        den = v1 + v2
        gate = p_grp * jnp.where(row == i1, v1 / den, jnp.where(row == i2, v2 / den, 0.0))

        onehot = (row == grp).astype(F32) * counted
        before = _dot(onehot.astype(BF16), utri_ref[...]) + carry[...]
        rank = jnp.sum(before * onehot, axis=0, keepdims=True)
        carry[...] = carry[...] + jnp.sum(onehot, axis=1, keepdims=True)
        cnt_ref[...] = jnp.broadcast_to(carry[...], (SUB, LANES))
        meta_ref[...] = jnp.where(row == META_GRP, grp, jnp.where(row == META_RANK, rank, 0.0))

        route_s[...] = jnp.concatenate([gate, jnp.zeros((LANES - SUB, TM), F32)], axis=0).T

    def back_tile_low():
        hx_ref[...] = jnp.zeros((TM * TOK_ROWS, LANES), F32)
        for c in range(ACT_ROWS // 2):
            hx_ref[_tile_rows(c, TM, TOK_ROWS), :] = h2_s[:, c * LANES:(c + 1) * LANES]

    def back_tile_high():
        for c in range(ACT_ROWS // 2, ACT_ROWS):
            hx_ref[_tile_rows(c, TM, TOK_ROWS), :] = h2_s[:, c * LANES:(c + 1) * LANES]
        hx_ref[_tile_rows(ROUTE_ROW, TM, TOK_ROWS), :] = route_s[...]

    back = (back_residual, back_router, back_tile_low, back_tile_high)
    assert len(back) == TM // CHUNK

    for c in range(TM // CHUNK):
        r0 = c * CHUNK
        rows = pl.ds(r0, CHUNK)
        cv = _causal_conv_chunk(ybuf, cw, r0) + convb_ref[...]
        cv = _group_layer_norm(cv, a64, clng_ref[...], clnb_ref[...])
        cat_w[rows, 0:CONV_WIDTH] = (cv * _sigmoid(cv)).astype(BF16)
        for j in range(GMLP_WIDTH // LANES):
            cols = slice(j * LANES, (j + 1) * LANES)
            vj = vn_s[rows, cols]
            rhs = jnp.concatenate([jnp.where(low, vj, 0.0).astype(BF16),
                                   jnp.where(low, 0.0, vj).astype(BF16)], axis=0)
            mixed = _dot(w2_ref[j], rhs) + bfull_ref[:, cols]
            cat_w[rows, CONV_WIDTH + j * LANES:CONV_WIDTH + (j + 1) * LANES] = (
                zu_s[rows, cols] * mixed).astype(BF16)
        back[c]()
    ybuf[0:HALO, :] = ybuf[TM:TM + HALO, :]
    cat_r[...] = cat_w[...]


def _mixer(x, mod3, prm):
    bsz, seq, _ = x.shape
    nt = seq // TM
    n_tiles = bsz * nt

    def cur(s):
        return jnp.minimum(s, n_tiles - 1)

    def prev(s):
        return jnp.maximum(s - 1, 0)

    const = lambda shape: pl.BlockSpec(shape, lambda s: (0,) * len(shape))
    in_specs = [
        pl.BlockSpec((1, TM, D_MODEL), lambda s: (cur(s) // nt, cur(s) % nt, 0)),
        pl.BlockSpec((1, TM, D_MODEL), lambda s: (prev(s) // nt, prev(s) % nt, 0)),
        pl.BlockSpec((1, SUB, D_MODEL), lambda s: (cur(s) // nt, 0, 0)),
        pl.BlockSpec((1, SUB, D_MODEL), lambda s: (prev(s) // nt, 0, 0)),
        const((1, D_MODEL)),
        const((D_MODEL, 4 * CONV_WIDTH)),
        const((HALO, CONV_WIDTH)),
        const((1, CONV_WIDTH)), const((1, CONV_WIDTH)), const((1, CONV_WIDTH)),
        const((1, GMLP_WIDTH)), const((1, GMLP_WIDTH)),
        const((CONV_WIDTH, CONV_WIDTH)),
        const((GMLP_WIDTH // LANES, CHUNK, 2 * CHUNK)),
        const((CHUNK, GMLP_WIDTH)),
        const((D_MODEL, D_MODEL)),
        const((1, D_MODEL)),
        const((ROUTER_ROWS, D_MODEL)), const((ROUTER_ROWS, D_MODEL)), const((ROUTER_ROWS, 1)),
        const((TM, TM)),
    ]
    out_shape = (
        jax.ShapeDtypeStruct((bsz, seq, D_MODEL), F32),
        jax.ShapeDtypeStruct((bsz * seq * TOK_ROWS, LANES), F32),
        jax.ShapeDtypeStruct((SUB, bsz * seq), F32),
        jax.ShapeDtypeStruct((SUB, LANES), F32),
    )
    out_specs = (
        pl.BlockSpec((1, TM, D_MODEL), lambda s: (prev(s) // nt, prev(s) % nt, 0)),
        pl.BlockSpec((TM * TOK_ROWS, LANES), lambda s: (prev(s), 0)),
        pl.BlockSpec((SUB, TM), lambda s: (0, prev(s))),
        pl.BlockSpec((SUB, LANES), lambda s: (0, 0)),
    )
    return pl.pallas_call(
        functools.partial(_mixer_kernel, tiles_per_seq=nt),
        out_shape=out_shape,
        grid=(n_tiles + 1,),
        in_specs=in_specs,
        out_specs=out_specs,
        scratch_shapes=[
            pltpu.VMEM((TM + HALO, CONV_WIDTH), F32),
            pltpu.VMEM((TM, GMLP_WIDTH), F32),
            pltpu.VMEM((TM, GMLP_WIDTH), F32),
            pltpu.VMEM((TM, D_MODEL), BF16),
            pltpu.VMEM((TM, D_MODEL), BF16),
            pltpu.VMEM((TM, D_MODEL), F32),
            pltpu.VMEM((TM, LANES), F32),
            pltpu.VMEM((SUB, 1), F32),
        ],
        compiler_params=pltpu.CompilerParams(
            dimension_semantics=("arbitrary",),
            vmem_limit_bytes=VMEM_LIMIT),
        name="mixer",
    )(x, x, mod3, mod3, *prm)


def _dispatch_kernel(pos_ref, tail_ref, nv_ref, hx_ref, wg_ref, wu_ref, wd_ref,
                     hs_ref, wgb_ref, wub_ref, wdb_ref, zbuf, zsem, sem):
    i = pl.program_id(0)

    wgb_ref[...] = wg_ref[...].astype(BF16)
    wub_ref[...] = wu_ref[...].astype(BF16)
    wdb_ref[...] = wd_ref[...].astype(BF16)

    @pl.when(i == 0)
    def _():
        zbuf[...] = jnp.zeros((TB * TOK_ROWS, LANES), F32)

        def clear(token):
            dst = pl.ds(pl.multiple_of(token * TOK_ROWS, TOK_ROWS), TB * TOK_ROWS)
            return pltpu.make_async_copy(zbuf, hs_ref.at[dst, :], zsem)

        for g in range(N_GROUPS):
            clear(tail_ref[g]).start()
        for g in range(N_GROUPS):
            clear(tail_ref[g]).wait()

        def clear_block(j, carry):
            clear(j * TB).start()
            clear(j * TB).wait()
            return carry

        lax.fori_loop(nv_ref[0], hs_ref.shape[0] // (TB * TOK_ROWS), clear_block, 0)

    def row_copy(r, p):
        return pltpu.make_async_copy(hx_ref.at[_token_tile(r, TOK_ROWS), :],
                                     hs_ref.at[_token_tile(p, TOK_ROWS), :], sem)

    _for_each_token(TS, lambda r, j: row_copy(r, pos_ref[i * TS + r]).start(priority=j % 2))
    _for_each_token(TS, lambda r, j: row_copy(0, 0).wait())


def _dispatch(pos, tails, n_valid, hx, wg, wu, wd, n_rows):
    t = hx.shape[0] // TOK_ROWS
    n_exp = wg.shape[0]
    assert t // TS == n_exp
    w_in = lambda shape: pl.BlockSpec((1,) + shape, lambda i, *_: (i, 0, 0))
    return pl.pallas_call(
        _dispatch_kernel,
        out_shape=(
            jax.ShapeDtypeStruct((n_rows * TOK_ROWS, LANES), F32),
            jax.ShapeDtypeStruct(wg.shape, BF16),
            jax.ShapeDtypeStruct(wu.shape, BF16),
            jax.ShapeDtypeStruct(wd.shape, BF16),
        ),
        grid_spec=pltpu.PrefetchScalarGridSpec(
            num_scalar_prefetch=3,
            grid=(n_exp,),
            in_specs=[
                pl.BlockSpec((TS * TOK_ROWS, LANES), lambda i, *_: (i, 0)),
                w_in(wg.shape[1:]), w_in(wu.shape[1:]), w_in(wd.shape[1:]),
            ],
            out_specs=(
                pl.BlockSpec(memory_space=pl.ANY),
                w_in(wg.shape[1:]), w_in(wu.shape[1:]), w_in(wd.shape[1:]),
            ),
            scratch_shapes=[
                pltpu.VMEM((TB * TOK_ROWS, LANES), F32),
                pltpu.SemaphoreType.DMA,
                pltpu.SemaphoreType.DMA,
            ]),
        compiler_params=pltpu.CompilerParams(
            dimension_semantics=("arbitrary",),
            vmem_limit_bytes=VMEM_LIMIT),
        name="dispatch",
    )(pos, tails, n_valid, hx, wg, wu, wd)


def _experts_kernel(bg_ref, nv_ref, hs_ref, wg_ref, wu_ref, wd_ref, ys_ref, xb, hid):
    i = pl.program_id(0)

    @pl.when(i < nv_ref[0])
    def _():
        for c in range(ACT_ROWS):
            xb[:, c * LANES:(c + 1) * LANES] = hs_ref[_tile_rows(c, TB, TOK_ROWS), :].astype(BF16)
        route = hs_ref[_tile_rows(ROUTE_ROW, TB, TOK_ROWS), :]
        x = xb[...]
        for e in range(EXPERTS_PER_GROUP):
            g = _dot(x, wg_ref[e])
            u = _dot(x, wu_ref[e])
            act = (g * _sigmoid(g)) * u * route[:, e:e + 1]
            hid[:, e * D_EXPERT:(e + 1) * D_EXPERT] = act.astype(BF16)
        y = _dot(hid[...], wd_ref[...])
        for c in range(SUB):
            ys_ref[_tile_rows(c, TB, SUB), :] = y[:, c * LANES:(c + 1) * LANES]

    @pl.when(i >= nv_ref[0])
    def _():
        ys_ref[...] = jnp.zeros((TB * SUB, LANES), F32)


def _experts(blk_grp, n_valid, hs, wg, wu, wd, n_blocks):
    def row_map(i, bg, nv):
        return (jnp.minimum(i, nv[0] - 1), 0)

    def out_map(i, bg, nv):
        return (i, 0)

    def w_map(i, bg, nv):
        return (bg[i], 0, 0)

    def wd_map(i, bg, nv):
        return (bg[i], 0)

    return pl.pallas_call(
        _experts_kernel,
        out_shape=jax.ShapeDtypeStruct((n_blocks * TB * SUB, LANES), F32),
        grid_spec=pltpu.PrefetchScalarGridSpec(
            num_scalar_prefetch=2,
            grid=(n_blocks,),
            in_specs=[
                pl.BlockSpec((TB * TOK_ROWS, LANES), row_map),
                pl.BlockSpec((EXPERTS_PER_GROUP, D_MODEL, D_EXPERT), w_map),
                pl.BlockSpec((EXPERTS_PER_GROUP, D_MODEL, D_EXPERT), w_map),
                pl.BlockSpec((EXPERTS_PER_GROUP * D_EXPERT, D_MODEL), wd_map),
            ],
            out_specs=pl.BlockSpec((TB * SUB, LANES), out_map),
            scratch_shapes=[pltpu.VMEM((TB, D_MODEL), BF16),
                            pltpu.VMEM((TB, EXPERTS_PER_GROUP * D_EXPERT), BF16)]),
        compiler_params=pltpu.CompilerParams(
            dimension_semantics=("arbitrary",),
            vmem_limit_bytes=VMEM_LIMIT),
        name="experts",
    )(blk_grp, n_valid, hs, wg, wu, wd)


def _combine_kernel(pos_ref, x1_ref, mod_ref, gfin_ref, ys_ref, o_ref, ybuf, sems):
    i = pl.program_id(0)
    n = pl.num_programs(0)

    def row_copy(slot, r, p):
        return pltpu.make_async_copy(ys_ref.at[_token_tile(p, SUB), :],
                                     ybuf.at[slot, _token_tile(r, SUB), :], sems.at[slot])

    def fetch(tile, slot):
        _for_each_token(
            TC, lambda r, j: row_copy(slot, r, pos_ref[tile * TC + r]).start(priority=j % 2))

    @pl.when(i == 0)
    def _():
        fetch(0, 0)

    @pl.when(i + 1 < n)
    def _():
        fetch(i + 1, (i + 1) % 2)

    slot = i % 2

    _for_each_token(TC, lambda r, j: row_copy(slot, 0, 0).wait())

    gt2 = mod_ref[0][5:6]
    ycur = ybuf.at[slot]
    x2 = []
    ssq = jnp.zeros((TC, 1), F32)
    for c in range(SUB):
        cols = slice(c * LANES, (c + 1) * LANES)
        v = x1_ref[:, cols] + gt2[:, cols] * ycur[_tile_rows(c, TC, SUB), :]
        ssq = ssq + jnp.sum(v * v, axis=-1, keepdims=True)
        x2.append(v)
    scale = lax.rsqrt(ssq / D_MODEL + EPS)
    for c in range(SUB):
        cols = slice(c * LANES, (c + 1) * LANES)
        o_ref[:, cols] = x2[c] * scale * gfin_ref[:, cols]


def _combine(pos, x1, mod3, g_final, ys, seq):
    t = x1.shape[0]
    per_batch = seq // TC
    return pl.pallas_call(
        _combine_kernel,
        out_shape=jax.ShapeDtypeStruct((t, D_MODEL), F32),
        grid_spec=pltpu.PrefetchScalarGridSpec(
            num_scalar_prefetch=1,
            grid=(t // TC,),
            in_specs=[
                pl.BlockSpec((TC, D_MODEL), lambda i, pos: (i, 0)),
                pl.BlockSpec((1, SUB, D_MODEL), lambda i, pos: (i // per_batch, 0, 0)),
                pl.BlockSpec((1, D_MODEL), lambda i, pos: (0, 0)),
                pl.BlockSpec(memory_space=pl.ANY),
            ],
            out_specs=pl.BlockSpec((TC, D_MODEL), lambda i, pos: (i, 0)),
            scratch_shapes=[
                pltpu.VMEM((2, TC * SUB, LANES), F32),
                pltpu.SemaphoreType.DMA((2,)),
            ]),
        compiler_params=pltpu.CompilerParams(dimension_semantics=("arbitrary",)),
        name="combine",
    )(pos, x1, mod3, g_final, ys)


def _split_bf16(w):
    hi = w.astype(BF16)
    return hi, (w - hi.astype(F32)).astype(BF16)


def kernel(x, c, w_ada, b_ada, g_mix, w_in, conv_w, conv_b, conv_ln_g, conv_ln_b, sgu_ln_g,
           sgu_ln_b, sgu_w, sgu_b, w_out, g_ffn, w_coarse, b_coarse, w_fine, b_fine, w_gate,
           w_up, w_down, g_final):
    bsz, seq, d = x.shape
    t = bsz * seq
    n_blocks = t // TB + N_GROUPS
    n_rows = (n_blocks + 1) * TB

    mod = _ada(c.T, w_ada[0], b_ada[0][None, :])
    mod3 = jnp.pad(mod.reshape(bsz, 6, d), ((0, 0), (0, 2), (0, 0)))

    tril = jnp.tril(jnp.ones((CHUNK, CHUNK), F32))
    ws = (sgu_w[0] * tril).astype(BF16)
    w2 = jnp.concatenate([ws[0::2], ws[1::2]], axis=2)
    bfull = jnp.repeat(sgu_b[0].T, HEAD_DIM, axis=1)
    ch = jnp.arange(CONV_WIDTH) // HEAD_DIM
    a64 = ((ch[:, None] == ch[None, :]).astype(F32) / HEAD_DIM).astype(BF16)
    convw = jnp.pad(conv_w[0], ((0, HALO - CONV_KSIZE), (0, 0)))
    wr = jnp.zeros((ROUTER_ROWS, d), F32)
    wr = wr.at[:N_GROUPS].set(w_coarse[0].T)
    wr = wr.at[SUB:SUB * (1 + N_GROUPS)].set(
        jnp.transpose(w_fine[0], (0, 2, 1)).reshape(N_GROUPS * EXPERTS_PER_GROUP, d))
    br = jnp.zeros((ROUTER_ROWS, 1), F32)
    br = br.at[:N_GROUPS, 0].set(b_coarse[0])
    br = br.at[SUB:SUB * (1 + N_GROUPS), 0].set(b_fine[0].reshape(-1))
    wrh, wrl = _split_bf16(wr)
    ri = jnp.arange(TM)
    utri = (ri[:, None] < ri[None, :]).astype(BF16)
    prm = (g_mix, w_in[0].astype(BF16), convw, conv_b, conv_ln_g[0].reshape(1, -1),
           conv_ln_b[0].reshape(1, -1), sgu_ln_g[0].reshape(1, -1), sgu_ln_b[0].reshape(1, -1),
           a64, w2, bfull, w_out[0].astype(BF16), g_ffn, wrh, wrl, br, utri)

    x1, hx, meta, cnt = _mixer(x, mod3, prm)

    counts = cnt[:N_GROUPS, 0].astype(jnp.int32)
    nblk = (counts + TB - 1) // TB
    blk_end = jnp.cumsum(nblk)
    blk_start = blk_end - nblk
    grp = meta[META_GRP].astype(jnp.int32)
    rank = meta[META_RANK].astype(jnp.int32)
    pos = blk_start[grp] * TB + rank
    tails = blk_start * TB + counts
    n_valid = blk_end[N_GROUPS - 1:]
    blk_ids = jnp.minimum(jnp.arange(n_blocks, dtype=jnp.int32), n_valid[0] - 1)
    blk_grp = jnp.sum((blk_ids[:, None] >= blk_end[None, :]).astype(jnp.int32), axis=1)

    hs, wg, wu, wd = _dispatch(pos, tails, n_valid, hx, w_gate[0], w_up[0], w_down[0], n_rows)
    ys = _experts(blk_grp, n_valid, hs, wg, wu,
                  wd.reshape(N_GROUPS * EXPERTS_PER_GROUP * D_EXPERT, d), n_blocks)
    out = _combine(pos, x1.reshape(t, d), mod3, g_final[None, :], ys, seq)
    return out.reshape(bsz, seq, d)
```

```python
import functools

import jax
import jax.numpy as jnp
from jax import lax
from jax.experimental import pallas as pl
from jax.experimental.pallas import tpu as pltpu

F32 = jnp.float32
BF16 = jnp.bfloat16

D_MODEL = 1024
CONV_WIDTH = 512
CONV_KSIZE = 31
GMLP_WIDTH = 512
HEAD_DIM = 64
CHUNK = 128
N_GROUPS = 4
EXPERTS_PER_GROUP = 8
D_EXPERT = 256
EPS = 1e-6
assert EXPERTS_PER_GROUP == 8

LANES = 128
SUB = 8
TM = 512
HALO = 32
TB = 256
TS = 512
TC = 512
ACT_ROWS = D_MODEL // LANES
ROUTE_ROW = ACT_ROWS
TOK_ROWS = 2 * SUB
META_GRP = 0
META_RANK = 1
ROUTER_ROWS = SUB * (1 + N_GROUPS) + SUB
DMA_UNROLL = 8
VMEM_LIMIT = 56 * 1024 * 1024


def _dot(a, b):
    return jnp.dot(a, b, preferred_element_type=F32)


def _sigmoid(x):
    return 0.5 * jnp.tanh(0.5 * x) + 0.5


def _gelu_tanh(x):
    c = 0.7978845608028654
    return 0.5 * x * (1.0 + jnp.tanh(c * (x + 0.044715 * (x * x * x))))


def _tile_rows(c, n, per_token):
    return pl.ds(c, n, stride=per_token)


def _token_tile(t, per_token):
    if isinstance(t, int):
        return pl.ds(t * per_token, per_token)
    return pl.ds(pl.multiple_of(t * per_token, per_token), per_token)


def _for_each_token(n, fn):
    def group(gi, carry):
        for j in range(DMA_UNROLL):
            fn(gi * DMA_UNROLL + j, j)
        return carry
    lax.fori_loop(0, n // DMA_UNROLL, group, 0)


def _ada_kernel(ct_ref, w_ref, b_ref, o_ref):
    ct = ct_ref[...]
    ca = ct * _sigmoid(ct)
    w = w_ref[...]
    for b in range(ct.shape[1]):
        o_ref[b:b + 1, :] = jnp.sum(w * ca[:, b:b + 1], axis=0, keepdims=True) + b_ref[...]


def _ada(c_t, w_ada, b_ada):
    n = w_ada.shape[1]
    bsz = c_t.shape[1]
    tn = 1024
    return pl.pallas_call(
        _ada_kernel,
        out_shape=jax.ShapeDtypeStruct((bsz, n), F32),
        grid=(n // tn,),
        in_specs=[
            pl.BlockSpec((D_MODEL, bsz), lambda j: (0, 0)),
            pl.BlockSpec((D_MODEL, tn), lambda j: (0, j)),
            pl.BlockSpec((1, tn), lambda j: (0, j)),
        ],
        out_specs=pl.BlockSpec((bsz, tn), lambda j: (0, j)),
        name="ada",
    )(c_t, w_ada, b_ada)


def _group_layer_norm(v, a64, g, b):
    mu = _dot(v.astype(BF16), a64)
    d = v - mu
    var = _dot((d * d).astype(BF16), a64)
    return d * lax.rsqrt(var + EPS) * g + b


def _causal_conv_chunk(ybuf, cw, r0):
    out = None
    for r in range(SUB):
        part = None
        for q in range((CONV_KSIZE - 1 - r) // SUB + 1):
            k = CONV_KSIZE - 1 - (SUB * q + r)
            seg = ybuf[pl.ds(r0 + HALO - SUB - SUB * q, CHUNK + SUB), :]
            term = seg * cw[k:k + 1, :]
            part = term if part is None else part + term
        shifted = part[SUB - r:SUB - r + CHUNK, :]
        out = shifted if out is None else out + shifted
    return out


def _mixer_kernel(x_ref, xp_ref, mod_ref, modp_ref, gmix_ref, win_ref, convw_ref, convb_ref,
                  clng_ref, clnb_ref, slng_ref, slnb_ref, a64_ref, w2_ref, bfull_ref, wout_ref,
                  gffn_ref, wrh_ref, wrl_ref, br_ref, utri_ref,
                  x1_ref, hx_ref, meta_ref, cnt_ref,
                  ybuf, zu_s, vn_s, cat_w, cat_r, h2_s, route_s, carry, *, tiles_per_seq):
    s = pl.program_id(0)
    n_tiles = pl.num_programs(0) - 1

    @pl.when(lax.rem(jnp.minimum(s, n_tiles - 1), tiles_per_seq) == 0)
    def _():
        ybuf[0:HALO, :] = jnp.zeros((HALO, CONV_WIDTH), F32)

    @pl.when(s == 0)
    def _():
        carry[...] = jnp.zeros((SUB, 1), F32)
        cat_r[...] = jnp.zeros((TM, D_MODEL), BF16)

    counted = jnp.where(s > 0, 1.0, 0.0)

    x = x_ref[0]
    mod = mod_ref[0]
    sh1, sc1 = mod[0:1], mod[1:2]

    ms = jnp.mean(x * x, axis=-1, keepdims=True)
    h = x * lax.rsqrt(ms + EPS) * (gmix_ref[...] * (1.0 + sc1)) + sh1
    hb = h.astype(BF16)
    a64 = a64_ref[...]

    ga = _dot(hb, win_ref[:, 0:CONV_WIDTH])
    gg = _dot(hb, win_ref[:, CONV_WIDTH:2 * CONV_WIDTH])
    ybuf[HALO:HALO + TM, :] = ga * _sigmoid(gg)
    off = 2 * CONV_WIDTH
    zu_s[...] = _gelu_tanh(_dot(hb, win_ref[:, off:off + GMLP_WIDTH]))
    zv = _gelu_tanh(_dot(hb, win_ref[:, off + GMLP_WIDTH:off + 2 * GMLP_WIDTH]))
    vn_s[...] = _group_layer_norm(zv, a64, slng_ref[...], slnb_ref[...])

    cw = convw_ref[...]
    low = lax.broadcasted_iota(jnp.int32, (CHUNK, LANES), 1) < HEAD_DIM

    def back_residual():
        gt1 = modp_ref[0][2:3]
        x1_ref[0] = xp_ref[0] + gt1 * _dot(cat_r[...], wout_ref[...])

    def back_router():
        modp = modp_ref[0]
        sh2, sc2 = modp[3:4], modp[4:5]
        x1 = x1_ref[0]
        ms2 = jnp.mean(x1 * x1, axis=-1, keepdims=True)
        h2 = x1 * lax.rsqrt(ms2 + EPS) * (gffn_ref[...] * (1.0 + sc2)) + sh2
        h2_s[...] = h2
        hh = h2.astype(BF16)
        hl = (h2 - hh.astype(F32)).astype(BF16)

        def nt_dot(w, a):
            return lax.dot_general(w, a, (((1,), (1,)), ((), ())), preferred_element_type=F32)

        lt = (nt_dot(wrh_ref[...], hh) + nt_dot(wrh_ref[...], hl) + nt_dot(wrl_ref[...], hh)
              + br_ref[...])
        row = lax.broadcasted_iota(jnp.int32, (SUB, TM), 0).astype(F32)
        neg = jnp.float32(-jnp.inf)
        far = jnp.float32(SUB)
        cm = row < N_GROUPS
        lc = lt[0:SUB]
        m = jnp.max(jnp.where(cm, lc, neg), axis=0, keepdims=True)
        ec = jnp.exp(jnp.where(cm, lc - m, neg))
        p_grp = 1.0 / jnp.sum(ec, axis=0, keepdims=True)
        grp = jnp.min(jnp.where(cm & (lc == m), row, far), axis=0, keepdims=True)

        lf = lt[SUB:2 * SUB]
        for g in range(1, N_GROUPS):
            lf = jnp.where(grp == g, lt[(g + 1) * SUB:(g + 2) * SUB], lf)
        mf = jnp.max(lf, axis=0, keepdims=True)
        ef = jnp.exp(lf - mf)
        fp = ef / jnp.sum(ef, axis=0, keepdims=True)
        v1 = jnp.max(fp, axis=0, keepdims=True)
        i1 = jnp.min(jnp.where(fp == v1, row, far), axis=0, keepdims=True)
        rest = row != i1
        v2 = jnp.max(jnp.where(rest, fp, -1.0), axis=0, keepdims=True)
        i2 = jnp.min(jnp.where(rest & (fp == v2), row, far), axis=0, keepdims=True)
        den = v1 + v2
        gate = p_grp * jnp.where(row == i1, v1 / den, jnp.where(row == i2, v2 / den, 0.0))

        onehot = (row == grp).astype(F32) * counted
        before = _dot(onehot.astype(BF16), utri_ref[...]) + carry[...]
        rank = jnp.sum(before * onehot, axis=0, keepdims=True)
        carry[...] = carry[...] + jnp.sum(onehot, axis=1, keepdims=True)
        cnt_ref[...] = jnp.broadcast_to(carry[...], (SUB, LANES))
        meta_ref[...] = jnp.where(row == META_GRP, grp, jnp.where(row == META_RANK, rank, 0.0))

        route_s[...] = jnp.concatenate([gate, jnp.zeros((LANES - SUB, TM), F32)], axis=0).T

    def back_tile_low():
        hx_ref[...] = jnp.zeros((TM * TOK_ROWS, LANES), F32)
        for c in range(ACT_ROWS // 2):
            hx_ref[_tile_rows(c, TM, TOK_ROWS), :] = h2_s[:, c * LANES:(c + 1) * LANES]

    def back_tile_high():
        for c in range(ACT_ROWS // 2, ACT_ROWS):
            hx_ref[_tile_rows(c, TM, TOK_ROWS), :] = h2_s[:, c * LANES:(c + 1) * LANES]
        hx_ref[_tile_rows(ROUTE_ROW, TM, TOK_ROWS), :] = route_s[...]

    back = (back_residual, back_router, back_tile_low, back_tile_high)
    assert len(back) == TM // CHUNK

    for c in range(TM // CHUNK):
        r0 = c * CHUNK
        rows = pl.ds(r0, CHUNK)
        cv = _causal_conv_chunk(ybuf, cw, r0) + convb_ref[...]
        cv = _group_layer_norm(cv, a64, clng_ref[...], clnb_ref[...])
        cat_w[rows, 0:CONV_WIDTH] = (cv * _sigmoid(cv)).astype(BF16)
        for j in range(GMLP_WIDTH // LANES):
            cols = slice(j * LANES, (j + 1) * LANES)
            vj = vn_s[rows, cols]
            rhs = jnp.concatenate([jnp.where(low, vj, 0.0).astype(BF16),
                                   jnp.where(low, 0.0, vj).astype(BF16)], axis=0)
            mixed = _dot(w2_ref[j], rhs) + bfull_ref[:, cols]
            cat_w[rows, CONV_WIDTH + j * LANES:CONV_WIDTH + (j + 1) * LANES] = (
                zu_s[rows, cols] * mixed).astype(BF16)
        back[c]()
    ybuf[0:HALO, :] = ybuf[TM:TM + HALO, :]
    cat_r[...] = cat_w[...]


def _mixer(x, mod3, prm):
    bsz, seq, _ = x.shape
    nt = seq // TM
    n_tiles = bsz * nt

    def cur(s):
        return jnp.minimum(s, n_tiles - 1)

    def prev(s):
        return jnp.maximum(s - 1, 0)

    const = lambda shape: pl.BlockSpec(shape, lambda s: (0,) * len(shape))
    in_specs = [
        pl.BlockSpec((1, TM, D_MODEL), lambda s: (cur(s) // nt, cur(s) % nt, 0)),
        pl.BlockSpec((1, TM, D_MODEL), lambda s: (prev(s) // nt, prev(s) % nt, 0)),
        pl.BlockSpec((1, SUB, D_MODEL), lambda s: (cur(s) // nt, 0, 0)),
        pl.BlockSpec((1, SUB, D_MODEL), lambda s: (prev(s) // nt, 0, 0)),
        const((1, D_MODEL)),
        const((D_MODEL, 4 * CONV_WIDTH)),
        const((HALO, CONV_WIDTH)),
        const((1, CONV_WIDTH)), const((1, CONV_WIDTH)), const((1, CONV_WIDTH)),
        const((1, GMLP_WIDTH)), const((1, GMLP_WIDTH)),
        const((CONV_WIDTH, CONV_WIDTH)),
        const((GMLP_WIDTH // LANES, CHUNK, 2 * CHUNK)),
        const((CHUNK, GMLP_WIDTH)),
        const((D_MODEL, D_MODEL)),
        const((1, D_MODEL)),
        const((ROUTER_ROWS, D_MODEL)), const((ROUTER_ROWS, D_MODEL)), const((ROUTER_ROWS, 1)),
        const((TM, TM)),
    ]
    out_shape = (
        jax.ShapeDtypeStruct((bsz, seq, D_MODEL), F32),
        jax.ShapeDtypeStruct((bsz * seq * TOK_ROWS, LANES), F32),
        jax.ShapeDtypeStruct((SUB, bsz * seq), F32),
        jax.ShapeDtypeStruct((SUB, LANES), F32),
    )
    out_specs = (
        pl.BlockSpec((1, TM, D_MODEL), lambda s: (prev(s) // nt, prev(s) % nt, 0)),
        pl.BlockSpec((TM * TOK_ROWS, LANES), lambda s: (prev(s), 0)),
        pl.BlockSpec((SUB, TM), lambda s: (0, prev(s))),
        pl.BlockSpec((SUB, LANES), lambda s: (0, 0)),
    )
    return pl.pallas_call(
        functools.partial(_mixer_kernel, tiles_per_seq=nt),
        out_shape=out_shape,
        grid=(n_tiles + 1,),
        in_specs=in_specs,
        out_specs=out_specs,
        scratch_shapes=[
            pltpu.VMEM((TM + HALO, CONV_WIDTH), F32),
            pltpu.VMEM((TM, GMLP_WIDTH), F32),
            pltpu.VMEM((TM, GMLP_WIDTH), F32),
            pltpu.VMEM((TM, D_MODEL), BF16),
            pltpu.VMEM((TM, D_MODEL), BF16),
            pltpu.VMEM((TM, D_MODEL), F32),
            pltpu.VMEM((TM, LANES), F32),
            pltpu.VMEM((SUB, 1), F32),
        ],
        compiler_params=pltpu.CompilerParams(
            dimension_semantics=("arbitrary",),
            vmem_limit_bytes=VMEM_LIMIT),
        name="mixer",
    )(x, x, mod3, mod3, *prm)


def _plan_kernel(pos_ref, lo_ref, hi_ref, wg_ref, wu_ref, wd_ref,
                 inv_ref, wgb_ref, wub_ref, wdb_ref):
    e = pl.program_id(0)
    wgb_ref[...] = wg_ref[...].astype(BF16)
    wub_ref[...] = wu_ref[...].astype(BF16)
    wdb_ref[...] = wd_ref[...].astype(BF16)

    @pl.when(e == 0)
    def _():
        def mark_unused(p, carry):
            inv_ref[p] = jnp.int32(-1)
            return carry
        for g in range(N_GROUPS + 1):
            lax.fori_loop(lo_ref[g], hi_ref[g], mark_unused, 0)

    def place(r, j):
        t = e * TS + r
        inv_ref[pos_ref[t]] = t

    _for_each_token(TS, place)


def _plan(pos, lo, hi, wg, wu, wd, n_slots):
    n_exp = wg.shape[0]
    assert pos.shape[0] // TS == n_exp
    w_blk = lambda shape: pl.BlockSpec((1,) + shape, lambda i, *_: (i, 0, 0))
    return pl.pallas_call(
        _plan_kernel,
        out_shape=(
            jax.ShapeDtypeStruct((n_slots,), jnp.int32),
            jax.ShapeDtypeStruct(wg.shape, BF16),
            jax.ShapeDtypeStruct(wu.shape, BF16),
            jax.ShapeDtypeStruct(wd.shape, BF16),
        ),
        grid_spec=pltpu.PrefetchScalarGridSpec(
            num_scalar_prefetch=3,
            grid=(n_exp,),
            in_specs=[w_blk(wg.shape[1:]), w_blk(wu.shape[1:]), w_blk(wd.shape[1:])],
            out_specs=(
                pl.BlockSpec(memory_space=pltpu.SMEM),
                w_blk(wg.shape[1:]), w_blk(wu.shape[1:]), w_blk(wd.shape[1:]),
            )),
        compiler_params=pltpu.CompilerParams(dimension_semantics=("arbitrary",)),
        name="plan",
    )(pos, lo, hi, wg, wu, wd)


def _experts_kernel(bg_ref, nv_ref, inv_ref, hx_ref, wg_ref, wu_ref, wd_ref, yt_ref,
                    xin, yout, xb, hid, gsem, ssem, *, n_tokens):
    i = pl.program_id(0)
    nv = nv_ref[0]
    slot = lax.rem(i, 2)
    other = 1 - slot

    def gather(blk, r, buf):
        tok = jnp.maximum(inv_ref[blk * TB + r], 0)
        return pltpu.make_async_copy(hx_ref.at[_token_tile(tok, TOK_ROWS), :],
                                     xin.at[buf, _token_tile(r, TOK_ROWS), :], gsem.at[buf])

    def scatter(blk, r, buf):
        tok = inv_ref[jnp.maximum(blk, 0) * TB + r]
        dst = jnp.where((tok < 0) | (blk < 0), n_tokens + buf * TB + r, tok)
        return pltpu.make_async_copy(yout.at[buf, _token_tile(r, SUB), :],
                                     yt_ref.at[_token_tile(dst, SUB), :], ssem.at[buf])

    @pl.when(i == 0)
    def _():
        xin[...] = jnp.zeros(xin.shape, F32)
        yout[...] = jnp.zeros(yout.shape, F32)
        for b in range(2):
            spare = pl.ds((n_tokens + b * TB) * SUB, TB * SUB)
            clear = pltpu.make_async_copy(yout.at[b], yt_ref.at[spare, :], ssem.at[b])
            clear.start()
            clear.wait()
        _for_each_token(TB, lambda r, j: gather(0, r, 0).start(priority=j % 2))

    @pl.when(i < nv)
    def _():
        _for_each_token(TB, lambda r, j: gather(0, 0, slot).wait())

    @pl.when((i < nv) & (i >= 1))
    def _():
        _for_each_token(TB, lambda r, j: scatter(0, 0, slot).wait())

    @pl.when(i < nv)
    def _():
        nxt = jnp.minimum(i + 1, nv - 1)
        prv = i - 1
        xs = xin.at[slot]
        for c in range(ACT_ROWS):
            xb[:, c * LANES:(c + 1) * LANES] = xs[_tile_rows(c, TB, TOK_ROWS), :].astype(BF16)
        route = xs[_tile_rows(ROUTE_ROW, TB, TOK_ROWS), :]
        x = xb[...]
        per_expert = TB // EXPERTS_PER_GROUP
        for e in range(EXPERTS_PER_GROUP):
            g = _dot(x, wg_ref[e])
            u = _dot(x, wu_ref[e])
            act = (g * _sigmoid(g)) * u * route[:, e:e + 1]
            hid[:, e * D_EXPERT:(e + 1) * D_EXPERT] = act.astype(BF16)
            for r in range(e * per_expert, (e + 1) * per_expert):
                gather(nxt, r, other).start(priority=r % 2)
                scatter(prv, r, other).start(priority=(r + 1) % 2)
        y = _dot(hid[...], wd_ref[...])
        ys = yout.at[slot]
        for c in range(SUB):
            ys[_tile_rows(c, TB, SUB), :] = y[:, c * LANES:(c + 1) * LANES]

    @pl.when(i == nv - 1)
    def _():
        _for_each_token(TB, lambda r, j: scatter(i, r, slot).start(priority=j % 2))
        _for_each_token(TB, lambda r, j: scatter(0, 0, other).wait())
        _for_each_token(TB, lambda r, j: scatter(0, 0, slot).wait())
        _for_each_token(TB, lambda r, j: gather(0, 0, other).wait())


def _experts(blk_grp, n_valid, inv, hx, wg, wu, wd, n_blocks):
    n_tokens = hx.shape[0] // TOK_ROWS

    def w_map(i, bg, nv, inv_):
        return (bg[i], 0, 0)

    def wd_map(i, bg, nv, inv_):
        return (bg[i], 0)

    return pl.pallas_call(
        functools.partial(_experts_kernel, n_tokens=n_tokens),
        out_shape=jax.ShapeDtypeStruct(((n_tokens + 2 * TB) * SUB, LANES), F32),
        grid_spec=pltpu.PrefetchScalarGridSpec(
            num_scalar_prefetch=3,
            grid=(n_blocks,),
            in_specs=[
                pl.BlockSpec(memory_space=pl.ANY),
                pl.BlockSpec((EXPERTS_PER_GROUP, D_MODEL, D_EXPERT), w_map),
                pl.BlockSpec((EXPERTS_PER_GROUP, D_MODEL, D_EXPERT), w_map),
                pl.BlockSpec((EXPERTS_PER_GROUP * D_EXPERT, D_MODEL), wd_map),
            ],
            out_specs=pl.BlockSpec(memory_space=pl.ANY),
            scratch_shapes=[
                pltpu.VMEM((2, TB * TOK_ROWS, LANES), F32),
                pltpu.VMEM((2, TB * SUB, LANES), F32),
                pltpu.VMEM((TB, D_MODEL), BF16),
                pltpu.VMEM((TB, EXPERTS_PER_GROUP * D_EXPERT), BF16),
                pltpu.SemaphoreType.DMA((2,)),
                pltpu.SemaphoreType.DMA((2,)),
            ]),
        compiler_params=pltpu.CompilerParams(
            dimension_semantics=("arbitrary",),
            vmem_limit_bytes=VMEM_LIMIT),
        name="experts",
    )(blk_grp, n_valid, inv, hx, wg, wu, wd)


def _combine_kernel(x1_ref, mod_ref, gfin_ref, yt_ref, o_ref):
    gt2 = mod_ref[0][5:6]
    x2 = []
    ssq = jnp.zeros((TC, 1), F32)
    for c in range(SUB):
        cols = slice(c * LANES, (c + 1) * LANES)
        v = x1_ref[:, cols] + gt2[:, cols] * yt_ref[_tile_rows(c, TC, SUB), :]
        ssq = ssq + jnp.sum(v * v, axis=-1, keepdims=True)
        x2.append(v)
    scale = lax.rsqrt(ssq / D_MODEL + EPS)
    for c in range(SUB):
        cols = slice(c * LANES, (c + 1) * LANES)
        o_ref[:, cols] = x2[c] * scale * gfin_ref[:, cols]


def _combine(x1, mod3, g_final, yt, seq):
    t = x1.shape[0]
    per_batch = seq // TC
    return pl.pallas_call(
        _combine_kernel,
        out_shape=jax.ShapeDtypeStruct((t, D_MODEL), F32),
        grid=(t // TC,),
        in_specs=[
            pl.BlockSpec((TC, D_MODEL), lambda i: (i, 0)),
            pl.BlockSpec((1, SUB, D_MODEL), lambda i: (i // per_batch, 0, 0)),
            pl.BlockSpec((1, D_MODEL), lambda i: (0, 0)),
            pl.BlockSpec((TC * SUB, LANES), lambda i: (i, 0)),
        ],
        out_specs=pl.BlockSpec((TC, D_MODEL), lambda i: (i, 0)),
        compiler_params=pltpu.CompilerParams(dimension_semantics=("arbitrary",)),
        name="combine",
    )(x1, mod3, g_final, yt)


def _split_bf16(w):
    hi = w.astype(BF16)
    return hi, (w - hi.astype(F32)).astype(BF16)


def kernel(x, c, w_ada, b_ada, g_mix, w_in, conv_w, conv_b, conv_ln_g, conv_ln_b, sgu_ln_g,
           sgu_ln_b, sgu_w, sgu_b, w_out, g_ffn, w_coarse, b_coarse, w_fine, b_fine, w_gate,
           w_up, w_down, g_final):
    bsz, seq, d = x.shape
    t = bsz * seq
    n_blocks = t // TB + N_GROUPS

    mod = _ada(c.T, w_ada[0], b_ada[0][None, :])
    mod3 = jnp.pad(mod.reshape(bsz, 6, d), ((0, 0), (0, 2), (0, 0)))

    tril = jnp.tril(jnp.ones((CHUNK, CHUNK), F32))
    ws = (sgu_w[0] * tril).astype(BF16)
    w2 = jnp.concatenate([ws[0::2], ws[1::2]], axis=2)
    bfull = jnp.repeat(sgu_b[0].T, HEAD_DIM, axis=1)
    ch = jnp.arange(CONV_WIDTH) // HEAD_DIM
    a64 = ((ch[:, None] == ch[None, :]).astype(F32) / HEAD_DIM).astype(BF16)
    convw = jnp.pad(conv_w[0], ((0, HALO - CONV_KSIZE), (0, 0)))
    wr = jnp.zeros((ROUTER_ROWS, d), F32)
    wr = wr.at[:N_GROUPS].set(w_coarse[0].T)
    wr = wr.at[SUB:SUB * (1 + N_GROUPS)].set(
        jnp.transpose(w_fine[0], (0, 2, 1)).reshape(N_GROUPS * EXPERTS_PER_GROUP, d))
    br = jnp.zeros((ROUTER_ROWS, 1), F32)
    br = br.at[:N_GROUPS, 0].set(b_coarse[0])
    br = br.at[SUB:SUB * (1 + N_GROUPS), 0].set(b_fine[0].reshape(-1))
    wrh, wrl = _split_bf16(wr)
    ri = jnp.arange(TM)
    utri = (ri[:, None] < ri[None, :]).astype(BF16)
    prm = (g_mix, w_in[0].astype(BF16), convw, conv_b, conv_ln_g[0].reshape(1, -1),
           conv_ln_b[0].reshape(1, -1), sgu_ln_g[0].reshape(1, -1), sgu_ln_b[0].reshape(1, -1),
           a64, w2, bfull, w_out[0].astype(BF16), g_ffn, wrh, wrl, br, utri)

    x1, hx, meta, cnt = _mixer(x, mod3, prm)

    counts = cnt[:N_GROUPS, 0].astype(jnp.int32)
    nblk = (counts + TB - 1) // TB
    blk_end = jnp.cumsum(nblk)
    blk_start = blk_end - nblk
    grp = meta[META_GRP].astype(jnp.int32)
    rank = meta[META_RANK].astype(jnp.int32)
    pos = blk_start[grp] * TB + rank
    n_valid = blk_end[N_GROUPS - 1:]
    unused_lo = jnp.concatenate([blk_start * TB + counts, n_valid * TB])
    unused_hi = jnp.concatenate([blk_end * TB, jnp.full((1,), n_blocks * TB, jnp.int32)])
    blk_ids = jnp.minimum(jnp.arange(n_blocks, dtype=jnp.int32), n_valid[0] - 1)
    blk_grp = jnp.sum((blk_ids[:, None] >= blk_end[None, :]).astype(jnp.int32), axis=1)

    inv, wg, wu, wd = _plan(pos, unused_lo, unused_hi, w_gate[0], w_up[0], w_down[0],
                            n_blocks * TB)
    yt = _experts(blk_grp, n_valid, inv, hx, wg, wu,
                  wd.reshape(N_GROUPS * EXPERTS_PER_GROUP * D_EXPERT, d), n_blocks)
    out = _combine(x1.reshape(t, d), mod3, g_final[None, :], yt, seq)
    return out.reshape(bsz, seq, d)
```

```python
import functools

import jax
import jax.numpy as jnp
from jax import lax
from jax.experimental import pallas as pl
from jax.experimental.pallas import tpu as pltpu

F32 = jnp.float32
BF16 = jnp.bfloat16

D_MODEL = 1024
CONV_WIDTH = 512
CONV_KSIZE = 31
GMLP_WIDTH = 512
HEAD_DIM = 64
CHUNK = 128
N_GROUPS = 4
EXPERTS_PER_GROUP = 8
D_EXPERT = 256
EPS = 1e-6
assert EXPERTS_PER_GROUP == 8

LANES = 128
SUB = 8
TM = 512
HALO = 32
TB = 256
TC = 512
ACT_ROWS = D_MODEL // LANES
ROUTE_ROW = ACT_ROWS
TOK_ROWS = 2 * SUB
META_GRP = 0
META_RANK = 1
ROUTER_ROWS = SUB * (1 + N_GROUPS) + SUB
DMA_UNROLL = 8
RING = 3
VMEM_LIMIT = 56 * 1024 * 1024


def _dot(a, b):
    return jnp.dot(a, b, preferred_element_type=F32)


def _sigmoid(x):
    return 0.5 * jnp.tanh(0.5 * x) + 0.5


def _gelu_tanh(x):
    c = 0.7978845608028654
    return 0.5 * x * (1.0 + jnp.tanh(c * (x + 0.044715 * (x * x * x))))


def _tile_rows(c, n, per_token):
    return pl.ds(c, n, stride=per_token)


def _token_tile(t, per_token):
    if isinstance(t, int):
        return pl.ds(t * per_token, per_token)
    return pl.ds(pl.multiple_of(t * per_token, per_token), per_token)


def _for_each_token(n, fn):
    def group(gi, carry):
        for j in range(DMA_UNROLL):
            fn(gi * DMA_UNROLL + j, j)
        return carry
    lax.fori_loop(0, n // DMA_UNROLL, group, 0)


def _ada_kernel(ct_ref, w_ref, b_ref, o_ref):
    ct = ct_ref[...]
    ca = ct * _sigmoid(ct)
    w = w_ref[...]
    for b in range(ct.shape[1]):
        o_ref[b:b + 1, :] = jnp.sum(w * ca[:, b:b + 1], axis=0, keepdims=True) + b_ref[...]


def _ada(c_t, w_ada, b_ada):
    n = w_ada.shape[1]
    bsz = c_t.shape[1]
    tn = 1024
    return pl.pallas_call(
        _ada_kernel,
        out_shape=jax.ShapeDtypeStruct((bsz, n), F32),
        grid=(n // tn,),
        in_specs=[
            pl.BlockSpec((D_MODEL, bsz), lambda j: (0, 0)),
            pl.BlockSpec((D_MODEL, tn), lambda j: (0, j)),
            pl.BlockSpec((1, tn), lambda j: (0, j)),
        ],
        out_specs=pl.BlockSpec((bsz, tn), lambda j: (0, j)),
        name="ada",
    )(c_t, w_ada, b_ada)


def _group_layer_norm(v, a64, g, b):
    mu = _dot(v.astype(BF16), a64)
    d = v - mu
    var = _dot((d * d).astype(BF16), a64)
    return d * lax.rsqrt(var + EPS) * g + b


def _causal_conv_chunk(ybuf, cw, r0):
    out = None
    for r in range(SUB):
        part = None
        for q in range((CONV_KSIZE - 1 - r) // SUB + 1):
            k = CONV_KSIZE - 1 - (SUB * q + r)
            seg = ybuf[pl.ds(r0 + HALO - SUB - SUB * q, CHUNK + SUB), :]
            term = seg * cw[k:k + 1, :]
            part = term if part is None else part + term
        shifted = part[SUB - r:SUB - r + CHUNK, :]
        out = shifted if out is None else out + shifted
    return out


def _mixer_kernel(x_ref, xp_ref, mod_ref, modp_ref, gmix_ref, win_ref, convw_ref, convb_ref,
                  clng_ref, clnb_ref, slng_ref, slnb_ref, a64_ref, w2_ref, bfull_ref, wout_ref,
                  gffn_ref, wrh_ref, wrl_ref, br_ref, utri_ref, wg_ref, wu_ref, wd_ref,
                  x1_ref, hx_ref, meta_ref, cnt_ref, wgb_ref, wub_ref, wdb_ref,
                  ybuf, zu_s, vn_s, cat_w, cat_r, h2_s, route_s, carry, *, tiles_per_seq):
    s = pl.program_id(0)
    n_tiles = pl.num_programs(0) - 1

    @pl.when(lax.rem(jnp.minimum(s, n_tiles - 1), tiles_per_seq) == 0)
    def _():
        ybuf[0:HALO, :] = jnp.zeros((HALO, CONV_WIDTH), F32)

    @pl.when(s == 0)
    def _():
        carry[...] = jnp.zeros((SUB, 1), F32)
        cat_r[...] = jnp.zeros((TM, D_MODEL), BF16)

    counted = jnp.where(s > 0, 1.0, 0.0)

    wgb_ref[...] = wg_ref[...].astype(BF16)
    wub_ref[...] = wu_ref[...].astype(BF16)
    wdb_ref[...] = wd_ref[...].astype(BF16)

    x = x_ref[0]
    mod = mod_ref[0]
    sh1, sc1 = mod[0:1], mod[1:2]

    ms = jnp.mean(x * x, axis=-1, keepdims=True)
    h = x * lax.rsqrt(ms + EPS) * (gmix_ref[...] * (1.0 + sc1)) + sh1
    hb = h.astype(BF16)
    a64 = a64_ref[...]

    ga = _dot(hb, win_ref[:, 0:CONV_WIDTH])
    gg = _dot(hb, win_ref[:, CONV_WIDTH:2 * CONV_WIDTH])
    ybuf[HALO:HALO + TM, :] = ga * _sigmoid(gg)
    off = 2 * CONV_WIDTH
    zu_s[...] = _gelu_tanh(_dot(hb, win_ref[:, off:off + GMLP_WIDTH]))
    zv = _gelu_tanh(_dot(hb, win_ref[:, off + GMLP_WIDTH:off + 2 * GMLP_WIDTH]))
    vn_s[...] = _group_layer_norm(zv, a64, slng_ref[...], slnb_ref[...])

    cw = convw_ref[...]
    low = lax.broadcasted_iota(jnp.int32, (CHUNK, LANES), 1) < HEAD_DIM

    def back_residual():
        gt1 = modp_ref[0][2:3]
        x1_ref[0] = xp_ref[0] + gt1 * _dot(cat_r[...], wout_ref[...])

    def back_router():
        modp = modp_ref[0]
        sh2, sc2 = modp[3:4], modp[4:5]
        x1 = x1_ref[0]
        ms2 = jnp.mean(x1 * x1, axis=-1, keepdims=True)
        h2 = x1 * lax.rsqrt(ms2 + EPS) * (gffn_ref[...] * (1.0 + sc2)) + sh2
        h2_s[...] = h2
        hh = h2.astype(BF16)
        hl = (h2 - hh.astype(F32)).astype(BF16)

        def nt_dot(w, a):
            return lax.dot_general(w, a, (((1,), (1,)), ((), ())), preferred_element_type=F32)

        lt = (nt_dot(wrh_ref[...], hh) + nt_dot(wrh_ref[...], hl) + nt_dot(wrl_ref[...], hh)
              + br_ref[...])
        row = lax.broadcasted_iota(jnp.int32, (SUB, TM), 0).astype(F32)
        neg = jnp.float32(-jnp.inf)
        far = jnp.float32(SUB)
        cm = row < N_GROUPS
        lc = lt[0:SUB]
        m = jnp.max(jnp.where(cm, lc, neg), axis=0, keepdims=True)
        ec = jnp.exp(jnp.where(cm, lc - m, neg))
        p_grp = 1.0 / jnp.sum(ec, axis=0, keepdims=True)
        grp = jnp.min(jnp.where(cm & (lc == m), row, far), axis=0, keepdims=True)

        lf = lt[SUB:2 * SUB]
        for g in range(1, N_GROUPS):
            lf = jnp.where(grp == g, lt[(g + 1) * SUB:(g + 2) * SUB], lf)
        mf = jnp.max(lf, axis=0, keepdims=True)
        ef = jnp.exp(lf - mf)
        fp = ef / jnp.sum(ef, axis=0, keepdims=True)
        v1 = jnp.max(fp, axis=0, keepdims=True)
        i1 = jnp.min(jnp.where(fp == v1, row, far), axis=0, keepdims=True)
        rest = row != i1
        v2 = jnp.max(jnp.where(rest, fp, -1.0), axis=0, keepdims=True)
        i2 = jnp.min(jnp.where(rest & (fp == v2), row, far), axis=0, keepdims=True)
        den = v1 + v2
        gate = p_grp * jnp.where(row == i1, v1 / den, jnp.where(row == i2, v2 / den, 0.0))

        onehot = (row == grp).astype(F32) * counted
        before = _dot(onehot.astype(BF16), utri_ref[...]) + carry[...]
        rank = jnp.sum(before * onehot, axis=0, keepdims=True)
        carry[...] = carry[...] + jnp.sum(onehot, axis=1, keepdims=True)
        cnt_ref[...] = jnp.broadcast_to(carry[...], (SUB, LANES))
        meta_ref[...] = jnp.where(row == META_GRP, grp, jnp.where(row == META_RANK, rank, 0.0))

        route_s[...] = jnp.concatenate([gate, jnp.zeros((LANES - SUB, TM), F32)], axis=0).T

    def back_tile_low():
        hx_ref[...] = jnp.zeros((TM * TOK_ROWS, LANES), F32)
        for c in range(ACT_ROWS // 2):
            hx_ref[_tile_rows(c, TM, TOK_ROWS), :] = h2_s[:, c * LANES:(c + 1) * LANES]

    def back_tile_high():
        for c in range(ACT_ROWS // 2, ACT_ROWS):
            hx_ref[_tile_rows(c, TM, TOK_ROWS), :] = h2_s[:, c * LANES:(c + 1) * LANES]
        hx_ref[_tile_rows(ROUTE_ROW, TM, TOK_ROWS), :] = route_s[...]

    back = (back_residual, back_router, back_tile_low, back_tile_high)
    assert len(back) == TM // CHUNK

    for c in range(TM // CHUNK):
        r0 = c * CHUNK
        rows = pl.ds(r0, CHUNK)
        cv = _causal_conv_chunk(ybuf, cw, r0) + convb_ref[...]
        cv = _group_layer_norm(cv, a64, clng_ref[...], clnb_ref[...])
        cat_w[rows, 0:CONV_WIDTH] = (cv * _sigmoid(cv)).astype(BF16)
        for j in range(GMLP_WIDTH // LANES):
            cols = slice(j * LANES, (j + 1) * LANES)
            vj = vn_s[rows, cols]
            rhs = jnp.concatenate([jnp.where(low, vj, 0.0).astype(BF16),
                                   jnp.where(low, 0.0, vj).astype(BF16)], axis=0)
            mixed = _dot(w2_ref[j], rhs) + bfull_ref[:, cols]
            cat_w[rows, CONV_WIDTH + j * LANES:CONV_WIDTH + (j + 1) * LANES] = (
                zu_s[rows, cols] * mixed).astype(BF16)
        back[c]()
    ybuf[0:HALO, :] = ybuf[TM:TM + HALO, :]
    cat_r[...] = cat_w[...]


def _mixer(x, mod3, prm, wg, wu, wd):
    bsz, seq, _ = x.shape
    nt = seq // TM
    n_tiles = bsz * nt
    n_exp = wg.shape[0]
    assert n_exp <= n_tiles + 1
    w_blk = lambda w: pl.BlockSpec((1,) + w.shape[1:], lambda s: (jnp.minimum(s, n_exp - 1), 0, 0))

    def cur(s):
        return jnp.minimum(s, n_tiles - 1)

    def prev(s):
        return jnp.maximum(s - 1, 0)

    const = lambda shape: pl.BlockSpec(shape, lambda s: (0,) * len(shape))
    in_specs = [
        pl.BlockSpec((1, TM, D_MODEL), lambda s: (cur(s) // nt, cur(s) % nt, 0)),
        pl.BlockSpec((1, TM, D_MODEL), lambda s: (prev(s) // nt, prev(s) % nt, 0)),
        pl.BlockSpec((1, SUB, D_MODEL), lambda s: (cur(s) // nt, 0, 0)),
        pl.BlockSpec((1, SUB, D_MODEL), lambda s: (prev(s) // nt, 0, 0)),
        const((1, D_MODEL)),
        const((D_MODEL, 4 * CONV_WIDTH)),
        const((HALO, CONV_WIDTH)),
        const((1, CONV_WIDTH)), const((1, CONV_WIDTH)), const((1, CONV_WIDTH)),
        const((1, GMLP_WIDTH)), const((1, GMLP_WIDTH)),
        const((CONV_WIDTH, CONV_WIDTH)),
        const((GMLP_WIDTH // LANES, CHUNK, 2 * CHUNK)),
        const((CHUNK, GMLP_WIDTH)),
        const((D_MODEL, D_MODEL)),
        const((1, D_MODEL)),
        const((ROUTER_ROWS, D_MODEL)), const((ROUTER_ROWS, D_MODEL)), const((ROUTER_ROWS, 1)),
        const((TM, TM)),
        w_blk(wg), w_blk(wu), w_blk(wd),
    ]
    out_shape = (
        jax.ShapeDtypeStruct((bsz, seq, D_MODEL), F32),
        jax.ShapeDtypeStruct((bsz * seq * TOK_ROWS, LANES), F32),
        jax.ShapeDtypeStruct((SUB, bsz * seq), F32),
        jax.ShapeDtypeStruct((SUB, LANES), F32),
        jax.ShapeDtypeStruct(wg.shape, BF16),
        jax.ShapeDtypeStruct(wu.shape, BF16),
        jax.ShapeDtypeStruct(wd.shape, BF16),
    )
    out_specs = (
        pl.BlockSpec((1, TM, D_MODEL), lambda s: (prev(s) // nt, prev(s) % nt, 0)),
        pl.BlockSpec((TM * TOK_ROWS, LANES), lambda s: (prev(s), 0)),
        pl.BlockSpec((SUB, TM), lambda s: (0, prev(s))),
        pl.BlockSpec((SUB, LANES), lambda s: (0, 0)),
        w_blk(wg), w_blk(wu), w_blk(wd),
    )
    return pl.pallas_call(
        functools.partial(_mixer_kernel, tiles_per_seq=nt),
        out_shape=out_shape,
        grid=(n_tiles + 1,),
        in_specs=in_specs,
        out_specs=out_specs,
        scratch_shapes=[
            pltpu.VMEM((TM + HALO, CONV_WIDTH), F32),
            pltpu.VMEM((TM, GMLP_WIDTH), F32),
            pltpu.VMEM((TM, GMLP_WIDTH), F32),
            pltpu.VMEM((TM, D_MODEL), BF16),
            pltpu.VMEM((TM, D_MODEL), BF16),
            pltpu.VMEM((TM, D_MODEL), F32),
            pltpu.VMEM((TM, LANES), F32),
            pltpu.VMEM((SUB, 1), F32),
        ],
        compiler_params=pltpu.CompilerParams(
            dimension_semantics=("arbitrary",),
            vmem_limit_bytes=VMEM_LIMIT),
        name="mixer",
    )(x, x, mod3, mod3, *prm, wg, wu, wd)


def _plan_kernel(pos_ref, lo_ref, hi_ref, inv_ref):
    def mark_unused(p, carry):
        inv_ref[p] = jnp.int32(-1)
        return carry

    for g in range(N_GROUPS + 1):
        lax.fori_loop(lo_ref[g], hi_ref[g], mark_unused, 0)

    def place(t, j):
        inv_ref[pos_ref[t]] = t

    _for_each_token(pos_ref.shape[0], place)


def _plan(pos, lo, hi, n_slots):
    return pl.pallas_call(
        _plan_kernel,
        out_shape=jax.ShapeDtypeStruct((n_slots,), jnp.int32),
        grid_spec=pltpu.PrefetchScalarGridSpec(
            num_scalar_prefetch=3,
            grid=(1,),
            in_specs=[],
            out_specs=pl.BlockSpec(memory_space=pltpu.SMEM)),
        compiler_params=pltpu.CompilerParams(dimension_semantics=("arbitrary",)),
        name="plan",
    )(pos, lo, hi)


def _experts_kernel(bg_ref, nv_ref, inv_ref, hx_ref, wg_ref, wu_ref, wd_ref, yt_ref,
                    xin, yout, xb, hid, gsem, ssem, *, n_tokens):
    i = pl.program_id(0)
    nv = nv_ref[0]
    cur = lax.rem(i, RING)
    nxt = lax.rem(i + 1, RING)
    prv = lax.rem(i + 2, RING)

    def gather(blk, r, buf):
        tok = jnp.maximum(inv_ref[blk * TB + r], 0)
        return pltpu.make_async_copy(hx_ref.at[_token_tile(tok, TOK_ROWS), :],
                                     xin.at[buf, _token_tile(r, TOK_ROWS), :], gsem.at[buf])

    def scatter(blk, r, buf):
        tok = inv_ref[jnp.maximum(blk, 0) * TB + r]
        dst = jnp.where((tok < 0) | (blk < 0), n_tokens + buf * TB + r, tok)
        return pltpu.make_async_copy(yout.at[buf, _token_tile(r, SUB), :],
                                     yt_ref.at[_token_tile(dst, SUB), :], ssem.at[buf])

    def wait_gathers(buf):
        _for_each_token(TB, lambda r, j: gather(0, 0, buf).wait())

    def wait_scatters(buf):
        _for_each_token(TB, lambda r, j: scatter(0, 0, buf).wait())

    @pl.when(i == 0)
    def _():
        xin[...] = jnp.zeros(xin.shape, F32)
        yout[...] = jnp.zeros(yout.shape, F32)
        for b in range(RING):
            spare = pl.ds((n_tokens + b * TB) * SUB, TB * SUB)
            clear = pltpu.make_async_copy(yout.at[b], yt_ref.at[spare, :], ssem.at[b])
            clear.start()
            clear.wait()
        second = jnp.minimum(1, nv - 1)
        _for_each_token(TB, lambda r, j: gather(0, r, 0).start(priority=j % 2))
        _for_each_token(TB, lambda r, j: gather(second, r, 1).start(priority=j % 2))

    @pl.when(i < nv)
    def _():
        wait_gathers(cur)

    @pl.when((i < nv) & (i >= 2))
    def _():
        wait_scatters(cur)

    @pl.when(i < nv)
    def _():
        ahead = jnp.minimum(i + 2, nv - 1)
        xs = xin.at[cur]
        for c in range(ACT_ROWS):
            xb[:, c * LANES:(c + 1) * LANES] = xs[_tile_rows(c, TB, TOK_ROWS), :].astype(BF16)
        route = xs[_tile_rows(ROUTE_ROW, TB, TOK_ROWS), :]
        x = xb[...]
        per_expert = TB // EXPERTS_PER_GROUP
        for e in range(EXPERTS_PER_GROUP):
            g = _dot(x, wg_ref[e])
            u = _dot(x, wu_ref[e])
            act = (g * _sigmoid(g)) * u * route[:, e:e + 1]
            hid[:, e * D_EXPERT:(e + 1) * D_EXPERT] = act.astype(BF16)
            for r in range(e * per_expert, (e + 1) * per_expert):
                gather(ahead, r, prv).start(priority=r % 2)
                scatter(i - 1, r, prv).start(priority=(r + 1) % 2)
        y = _dot(hid[...], wd_ref[...])
        ys = yout.at[cur]
        for c in range(SUB):
            ys[_tile_rows(c, TB, SUB), :] = y[:, c * LANES:(c + 1) * LANES]

    @pl.when(i == nv - 1)
    def _():
        _for_each_token(TB, lambda r, j: scatter(i, r, cur).start(priority=j % 2))
        wait_scatters(prv)
        wait_scatters(cur)
        wait_gathers(nxt)
        wait_gathers(prv)

    @pl.when((i == nv - 1) & (i >= 1))
    def _():
        wait_scatters(nxt)


def _experts(blk_grp, n_valid, inv, hx, wg, wu, wd, n_blocks):
    n_tokens = hx.shape[0] // TOK_ROWS

    def w_map(i, bg, nv, inv_):
        return (bg[i], 0, 0)

    def wd_map(i, bg, nv, inv_):
        return (bg[i], 0)

    return pl.pallas_call(
        functools.partial(_experts_kernel, n_tokens=n_tokens),
        out_shape=jax.ShapeDtypeStruct(((n_tokens + RING * TB) * SUB, LANES), F32),
        grid_spec=pltpu.PrefetchScalarGridSpec(
            num_scalar_prefetch=3,
            grid=(n_blocks,),
            in_specs=[
                pl.BlockSpec(memory_space=pl.ANY),
                pl.BlockSpec((EXPERTS_PER_GROUP, D_MODEL, D_EXPERT), w_map),
                pl.BlockSpec((EXPERTS_PER_GROUP, D_MODEL, D_EXPERT), w_map),
                pl.BlockSpec((EXPERTS_PER_GROUP * D_EXPERT, D_MODEL), wd_map),
            ],
            out_specs=pl.BlockSpec(memory_space=pl.ANY),
            scratch_shapes=[
                pltpu.VMEM((RING, TB * TOK_ROWS, LANES), F32),
                pltpu.VMEM((RING, TB * SUB, LANES), F32),
                pltpu.VMEM((TB, D_MODEL), BF16),
                pltpu.VMEM((TB, EXPERTS_PER_GROUP * D_EXPERT), BF16),
                pltpu.SemaphoreType.DMA((RING,)),
                pltpu.SemaphoreType.DMA((RING,)),
            ]),
        compiler_params=pltpu.CompilerParams(
            dimension_semantics=("arbitrary",),
            vmem_limit_bytes=VMEM_LIMIT),
        name="experts",
    )(blk_grp, n_valid, inv, hx, wg, wu, wd)


def _combine_kernel(x1_ref, mod_ref, gfin_ref, yt_ref, o_ref):
    gt2 = mod_ref[0][5:6]
    x2 = []
    ssq = jnp.zeros((TC, 1), F32)
    for c in range(SUB):
        cols = slice(c * LANES, (c + 1) * LANES)
        v = x1_ref[:, cols] + gt2[:, cols] * yt_ref[_tile_rows(c, TC, SUB), :]
        ssq = ssq + jnp.sum(v * v, axis=-1, keepdims=True)
        x2.append(v)
    scale = lax.rsqrt(ssq / D_MODEL + EPS)
    for c in range(SUB):
        cols = slice(c * LANES, (c + 1) * LANES)
        o_ref[:, cols] = x2[c] * scale * gfin_ref[:, cols]


def _combine(x1, mod3, g_final, yt, seq):
    t = x1.shape[0]
    per_batch = seq // TC
    return pl.pallas_call(
        _combine_kernel,
        out_shape=jax.ShapeDtypeStruct((t, D_MODEL), F32),
        grid=(t // TC,),
        in_specs=[
            pl.BlockSpec((TC, D_MODEL), lambda i: (i, 0)),
            pl.BlockSpec((1, SUB, D_MODEL), lambda i: (i // per_batch, 0, 0)),
            pl.BlockSpec((1, D_MODEL), lambda i: (0, 0)),
            pl.BlockSpec((TC * SUB, LANES), lambda i: (i, 0)),
        ],
        out_specs=pl.BlockSpec((TC, D_MODEL), lambda i: (i, 0)),
        compiler_params=pltpu.CompilerParams(dimension_semantics=("arbitrary",)),
        name="combine",
    )(x1, mod3, g_final, yt)


def _split_bf16(w):
    hi = w.astype(BF16)
    return hi, (w - hi.astype(F32)).astype(BF16)


def kernel(x, c, w_ada, b_ada, g_mix, w_in, conv_w, conv_b, conv_ln_g, conv_ln_b, sgu_ln_g,
           sgu_ln_b, sgu_w, sgu_b, w_out, g_ffn, w_coarse, b_coarse, w_fine, b_fine, w_gate,
           w_up, w_down, g_final):
    bsz, seq, d = x.shape
    t = bsz * seq
    n_blocks = t // TB + N_GROUPS

    mod = _ada(c.T, w_ada[0], b_ada[0][None, :])
    mod3 = jnp.pad(mod.reshape(bsz, 6, d), ((0, 0), (0, 2), (0, 0)))

    tril = jnp.tril(jnp.ones((CHUNK, CHUNK), F32))
    ws = (sgu_w[0] * tril).astype(BF16)
    w2 = jnp.concatenate([ws[0::2], ws[1::2]], axis=2)
    bfull = jnp.repeat(sgu_b[0].T, HEAD_DIM, axis=1)
    ch = jnp.arange(CONV_WIDTH) // HEAD_DIM
    a64 = ((ch[:, None] == ch[None, :]).astype(F32) / HEAD_DIM).astype(BF16)
    convw = jnp.pad(conv_w[0], ((0, HALO - CONV_KSIZE), (0, 0)))
    wr = jnp.zeros((ROUTER_ROWS, d), F32)
    wr = wr.at[:N_GROUPS].set(w_coarse[0].T)
    wr = wr.at[SUB:SUB * (1 + N_GROUPS)].set(
        jnp.transpose(w_fine[0], (0, 2, 1)).reshape(N_GROUPS * EXPERTS_PER_GROUP, d))
    br = jnp.zeros((ROUTER_ROWS, 1), F32)
    br = br.at[:N_GROUPS, 0].set(b_coarse[0])
    br = br.at[SUB:SUB * (1 + N_GROUPS), 0].set(b_fine[0].reshape(-1))
    wrh, wrl = _split_bf16(wr)
    ri = jnp.arange(TM)
    utri = (ri[:, None] < ri[None, :]).astype(BF16)
    prm = (g_mix, w_in[0].astype(BF16), convw, conv_b, conv_ln_g[0].reshape(1, -1),
           conv_ln_b[0].reshape(1, -1), sgu_ln_g[0].reshape(1, -1), sgu_ln_b[0].reshape(1, -1),
           a64, w2, bfull, w_out[0].astype(BF16), g_ffn, wrh, wrl, br, utri)

    x1, hx, meta, cnt, wg, wu, wd = _mixer(x, mod3, prm, w_gate[0], w_up[0], w_down[0])

    counts = cnt[:N_GROUPS, 0].astype(jnp.int32)
    nblk = (counts + TB - 1) // TB
    blk_end = jnp.cumsum(nblk)
    blk_start = blk_end - nblk
    grp = meta[META_GRP].astype(jnp.int32)
    rank = meta[META_RANK].astype(jnp.int32)
    pos = blk_start[grp] * TB + rank
    n_valid = blk_end[N_GROUPS - 1:]
    unused_lo = jnp.concatenate([blk_start * TB + counts, n_valid * TB])
    unused_hi = jnp.concatenate([blk_end * TB, jnp.full((1,), n_blocks * TB, jnp.int32)])
    blk_ids = jnp.minimum(jnp.arange(n_blocks, dtype=jnp.int32), n_valid[0] - 1)
    blk_grp = jnp.sum((blk_ids[:, None] >= blk_end[None, :]).astype(jnp.int32), axis=1)

    inv = _plan(pos, unused_lo, unused_hi, n_blocks * TB)
    yt = _experts(blk_grp, n_valid, inv, hx, wg, wu,
                  wd.reshape(N_GROUPS * EXPERTS_PER_GROUP * D_EXPERT, d), n_blocks)
    out = _combine(x1.reshape(t, d), mod3, g_final[None, :], yt, seq)
    return out.reshape(bsz, seq, d)
```

```python
import functools

import jax
import jax.numpy as jnp
from jax import lax
from jax.experimental import pallas as pl
from jax.experimental.pallas import tpu as pltpu

F32 = jnp.float32
BF16 = jnp.bfloat16

D_MODEL = 1024
CONV_WIDTH = 512
CONV_KSIZE = 31
GMLP_WIDTH = 512
HEAD_DIM = 64
CHUNK = 128
N_GROUPS = 4
EXPERTS_PER_GROUP = 8
D_EXPERT = 256
EPS = 1e-6
assert EXPERTS_PER_GROUP == 8

LANES = 128
SUB = 8
TM = 512
HALO = 32
TB = 256
TC = 512
ACT_ROWS = D_MODEL // LANES
ROUTE_ROW = ACT_ROWS
TOK_ROWS = 2 * SUB
META_GRP = 0
META_RANK = 1
ROUTER_ROWS = SUB * (1 + N_GROUPS) + SUB
DMA_UNROLL = 8
GRING = 4
SRING = 3
VMEM_LIMIT = 56 * 1024 * 1024


def _dot(a, b):
    return jnp.dot(a, b, preferred_element_type=F32)


def _sigmoid(x):
    return 0.5 * jnp.tanh(0.5 * x) + 0.5


def _gelu_tanh(x):
    c = 0.7978845608028654
    return 0.5 * x * (1.0 + jnp.tanh(c * (x + 0.044715 * (x * x * x))))


def _tile_rows(c, n, per_token):
    return pl.ds(c, n, stride=per_token)


def _token_tile(t, per_token):
    if isinstance(t, int):
        return pl.ds(t * per_token, per_token)
    return pl.ds(pl.multiple_of(t * per_token, per_token), per_token)


def _for_each_token(n, fn):
    def group(gi, carry):
        for j in range(DMA_UNROLL):
            fn(gi * DMA_UNROLL + j, j)
        return carry
    lax.fori_loop(0, n // DMA_UNROLL, group, 0)


def _ada_kernel(ct_ref, w_ref, b_ref, o_ref):
    ct = ct_ref[...]
    ca = ct * _sigmoid(ct)
    w = w_ref[...]
    for b in range(ct.shape[1]):
        o_ref[b:b + 1, :] = jnp.sum(w * ca[:, b:b + 1], axis=0, keepdims=True) + b_ref[...]


def _ada(c_t, w_ada, b_ada):
    n = w_ada.shape[1]
    bsz = c_t.shape[1]
    tn = 1024
    return pl.pallas_call(
        _ada_kernel,
        out_shape=jax.ShapeDtypeStruct((bsz, n), F32),
        grid=(n // tn,),
        in_specs=[
            pl.BlockSpec((D_MODEL, bsz), lambda j: (0, 0)),
            pl.BlockSpec((D_MODEL, tn), lambda j: (0, j)),
            pl.BlockSpec((1, tn), lambda j: (0, j)),
        ],
        out_specs=pl.BlockSpec((bsz, tn), lambda j: (0, j)),
        name="ada",
    )(c_t, w_ada, b_ada)


def _group_layer_norm(v, a64, g, b):
    mu = _dot(v.astype(BF16), a64)
    d = v - mu
    var = _dot((d * d).astype(BF16), a64)
    return d * lax.rsqrt(var + EPS) * g + b


def _causal_conv_chunk(ybuf, cw, r0):
    out = None
    for r in range(SUB):
        part = None
        for q in range((CONV_KSIZE - 1 - r) // SUB + 1):
            k = CONV_KSIZE - 1 - (SUB * q + r)
            seg = ybuf[pl.ds(r0 + HALO - SUB - SUB * q, CHUNK + SUB), :]
            term = seg * cw[k:k + 1, :]
            part = term if part is None else part + term
        shifted = part[SUB - r:SUB - r + CHUNK, :]
        out = shifted if out is None else out + shifted
    return out


def _mixer_kernel(x_ref, xp_ref, mod_ref, modp_ref, gmix_ref, win_ref, convw_ref, convb_ref,
                  clng_ref, clnb_ref, slng_ref, slnb_ref, a64_ref, w2_ref, bfull_ref, wout_ref,
                  gffn_ref, wrh_ref, wrl_ref, br_ref, utri_ref, wg_ref, wu_ref, wd_ref,
                  x1_ref, hx_ref, meta_ref, cnt_ref, wgb_ref, wub_ref, wdb_ref,
                  ybuf, zu_s, vn_s, cat_w, cat_r, h2_s, route_s, carry, *, tiles_per_seq):
    s = pl.program_id(0)
    n_tiles = pl.num_programs(0) - 1

    @pl.when(lax.rem(jnp.minimum(s, n_tiles - 1), tiles_per_seq) == 0)
    def _():
        ybuf[0:HALO, :] = jnp.zeros((HALO, CONV_WIDTH), F32)

    @pl.when(s == 0)
    def _():
        carry[...] = jnp.zeros((SUB, 1), F32)
        cat_r[...] = jnp.zeros((TM, D_MODEL), BF16)

    counted = jnp.where(s > 0, 1.0, 0.0)

    wgb_ref[...] = wg_ref[...].astype(BF16)
    wub_ref[...] = wu_ref[...].astype(BF16)
    wdb_ref[...] = wd_ref[...].astype(BF16)

    x = x_ref[0]
    mod = mod_ref[0]
    sh1, sc1 = mod[0:1], mod[1:2]

    ms = jnp.mean(x * x, axis=-1, keepdims=True)
    h = x * lax.rsqrt(ms + EPS) * (gmix_ref[...] * (1.0 + sc1)) + sh1
    hb = h.astype(BF16)
    a64 = a64_ref[...]

    ga = _dot(hb, win_ref[:, 0:CONV_WIDTH])
    gg = _dot(hb, win_ref[:, CONV_WIDTH:2 * CONV_WIDTH])
    ybuf[HALO:HALO + TM, :] = ga * _sigmoid(gg)
    off = 2 * CONV_WIDTH
    zu_s[...] = _gelu_tanh(_dot(hb, win_ref[:, off:off + GMLP_WIDTH]))
    zv = _gelu_tanh(_dot(hb, win_ref[:, off + GMLP_WIDTH:off + 2 * GMLP_WIDTH]))
    vn_s[...] = _group_layer_norm(zv, a64, slng_ref[...], slnb_ref[...])

    cw = convw_ref[...]
    low = lax.broadcasted_iota(jnp.int32, (CHUNK, LANES), 1) < HEAD_DIM

    def back_residual():
        gt1 = modp_ref[0][2:3]
        x1_ref[0] = xp_ref[0] + gt1 * _dot(cat_r[...], wout_ref[...])

    def back_router():
        modp = modp_ref[0]
        sh2, sc2 = modp[3:4], modp[4:5]
        x1 = x1_ref[0]
        ms2 = jnp.mean(x1 * x1, axis=-1, keepdims=True)
        h2 = x1 * lax.rsqrt(ms2 + EPS) * (gffn_ref[...] * (1.0 + sc2)) + sh2
        h2_s[...] = h2
        hh = h2.astype(BF16)
        hl = (h2 - hh.astype(F32)).astype(BF16)

        def nt_dot(w, a):
            return lax.dot_general(w, a, (((1,), (1,)), ((), ())), preferred_element_type=F32)

        lt = (nt_dot(wrh_ref[...], hh) + nt_dot(wrh_ref[...], hl) + nt_dot(wrl_ref[...], hh)
              + br_ref[...])
        row = lax.broadcasted_iota(jnp.int32, (SUB, TM), 0).astype(F32)
        neg = jnp.float32(-jnp.inf)
        far = jnp.float32(SUB)
        cm = row < N_GROUPS
        lc = lt[0:SUB]
        m = jnp.max(jnp.where(cm, lc, neg), axis=0, keepdims=True)
        ec = jnp.exp(jnp.where(cm, lc - m, neg))
        p_grp = 1.0 / jnp.sum(ec, axis=0, keepdims=True)
        grp = jnp.min(jnp.where(cm & (lc == m), row, far), axis=0, keepdims=True)

        lf = lt[SUB:2 * SUB]
        for g in range(1, N_GROUPS):
            lf = jnp.where(grp == g, lt[(g + 1) * SUB:(g + 2) * SUB], lf)
        mf = jnp.max(lf, axis=0, keepdims=True)
        ef = jnp.exp(lf - mf)
        fp = ef / jnp.sum(ef, axis=0, keepdims=True)
        v1 = jnp.max(fp, axis=0, keepdims=True)
        i1 = jnp.min(jnp.where(fp == v1, row, far), axis=0, keepdims=True)
        rest = row != i1
        v2 = jnp.max(jnp.where(rest, fp, -1.0), axis=0, keepdims=True)
        i2 = jnp.min(jnp.where(rest & (fp == v2), row, far), axis=0, keepdims=True)
        den = v1 + v2
        gate = p_grp * jnp.where(row == i1, v1 / den, jnp.where(row == i2, v2 / den, 0.0))

        onehot = (row == grp).astype(F32) * counted
        before = _dot(onehot.astype(BF16), utri_ref[...]) + carry[...]
        rank = jnp.sum(before * onehot, axis=0, keepdims=True)
        carry[...] = carry[...] + jnp.sum(onehot, axis=1, keepdims=True)
        cnt_ref[...] = jnp.broadcast_to(carry[...], (SUB, LANES))
        meta_ref[...] = jnp.where(row == META_GRP, grp, jnp.where(row == META_RANK, rank, 0.0))

        route_s[...] = jnp.concatenate([gate, jnp.zeros((LANES - SUB, TM), F32)], axis=0).T

    def back_tile_low():
        hx_ref[...] = jnp.zeros((TM * TOK_ROWS, LANES), F32)
        for c in range(ACT_ROWS // 2):
            hx_ref[_tile_rows(c, TM, TOK_ROWS), :] = h2_s[:, c * LANES:(c + 1) * LANES]

    def back_tile_high():
        for c in range(ACT_ROWS // 2, ACT_ROWS):
            hx_ref[_tile_rows(c, TM, TOK_ROWS), :] = h2_s[:, c * LANES:(c + 1) * LANES]
        hx_ref[_tile_rows(ROUTE_ROW, TM, TOK_ROWS), :] = route_s[...]

    back = (back_residual, back_router, back_tile_low, back_tile_high)
    assert len(back) == TM // CHUNK

    for c in range(TM // CHUNK):
        r0 = c * CHUNK
        rows = pl.ds(r0, CHUNK)
        cv = _causal_conv_chunk(ybuf, cw, r0) + convb_ref[...]
        cv = _group_layer_norm(cv, a64, clng_ref[...], clnb_ref[...])
        cat_w[rows, 0:CONV_WIDTH] = (cv * _sigmoid(cv)).astype(BF16)
        for j in range(GMLP_WIDTH // LANES):
            cols = slice(j * LANES, (j + 1) * LANES)
            vj = vn_s[rows, cols]
            rhs = jnp.concatenate([jnp.where(low, vj, 0.0).astype(BF16),
                                   jnp.where(low, 0.0, vj).astype(BF16)], axis=0)
            mixed = _dot(w2_ref[j], rhs) + bfull_ref[:, cols]
            cat_w[rows, CONV_WIDTH + j * LANES:CONV_WIDTH + (j + 1) * LANES] = (
                zu_s[rows, cols] * mixed).astype(BF16)
        back[c]()
    ybuf[0:HALO, :] = ybuf[TM:TM + HALO, :]
    cat_r[...] = cat_w[...]


def _mixer(x, mod3, prm, wg, wu, wd):
    bsz, seq, _ = x.shape
    nt = seq // TM
    n_tiles = bsz * nt
    n_exp = wg.shape[0]
    assert n_exp <= n_tiles + 1
    w_blk = lambda w: pl.BlockSpec((1,) + w.shape[1:], lambda s: (jnp.minimum(s, n_exp - 1), 0, 0))

    def cur(s):
        return jnp.minimum(s, n_tiles - 1)

    def prev(s):
        return jnp.maximum(s - 1, 0)

    const = lambda shape: pl.BlockSpec(shape, lambda s: (0,) * len(shape))
    in_specs = [
        pl.BlockSpec((1, TM, D_MODEL), lambda s: (cur(s) // nt, cur(s) % nt, 0)),
        pl.BlockSpec((1, TM, D_MODEL), lambda s: (prev(s) // nt, prev(s) % nt, 0)),
        pl.BlockSpec((1, SUB, D_MODEL), lambda s: (cur(s) // nt, 0, 0)),
        pl.BlockSpec((1, SUB, D_MODEL), lambda s: (prev(s) // nt, 0, 0)),
        const((1, D_MODEL)),
        const((D_MODEL, 4 * CONV_WIDTH)),
        const((HALO, CONV_WIDTH)),
        const((1, CONV_WIDTH)), const((1, CONV_WIDTH)), const((1, CONV_WIDTH)),
        const((1, GMLP_WIDTH)), const((1, GMLP_WIDTH)),
        const((CONV_WIDTH, CONV_WIDTH)),
        const((GMLP_WIDTH // LANES, CHUNK, 2 * CHUNK)),
        const((CHUNK, GMLP_WIDTH)),
        const((D_MODEL, D_MODEL)),
        const((1, D_MODEL)),
        const((ROUTER_ROWS, D_MODEL)), const((ROUTER_ROWS, D_MODEL)), const((ROUTER_ROWS, 1)),
        const((TM, TM)),
        w_blk(wg), w_blk(wu), w_blk(wd),
    ]
    out_shape = (
        jax.ShapeDtypeStruct((bsz, seq, D_MODEL), F32),
        jax.ShapeDtypeStruct((bsz * seq * TOK_ROWS, LANES), F32),
        jax.ShapeDtypeStruct((SUB, bsz * seq), F32),
        jax.ShapeDtypeStruct((SUB, LANES), F32),
        jax.ShapeDtypeStruct(wg.shape, BF16),
        jax.ShapeDtypeStruct(wu.shape, BF16),
        jax.ShapeDtypeStruct(wd.shape, BF16),
    )
    out_specs = (
        pl.BlockSpec((1, TM, D_MODEL), lambda s: (prev(s) // nt, prev(s) % nt, 0)),
        pl.BlockSpec((TM * TOK_ROWS, LANES), lambda s: (prev(s), 0)),
        pl.BlockSpec((SUB, TM), lambda s: (0, prev(s))),
        pl.BlockSpec((SUB, LANES), lambda s: (0, 0)),
        w_blk(wg), w_blk(wu), w_blk(wd),
    )
    return pl.pallas_call(
        functools.partial(_mixer_kernel, tiles_per_seq=nt),
        out_shape=out_shape,
        grid=(n_tiles + 1,),
        in_specs=in_specs,
        out_specs=out_specs,
        scratch_shapes=[
            pltpu.VMEM((TM + HALO, CONV_WIDTH), F32),
            pltpu.VMEM((TM, GMLP_WIDTH), F32),
            pltpu.VMEM((TM, GMLP_WIDTH), F32),
            pltpu.VMEM((TM, D_MODEL), BF16),
            pltpu.VMEM((TM, D_MODEL), BF16),
            pltpu.VMEM((TM, D_MODEL), F32),
            pltpu.VMEM((TM, LANES), F32),
            pltpu.VMEM((SUB, 1), F32),
        ],
        compiler_params=pltpu.CompilerParams(
            dimension_semantics=("arbitrary",),
            vmem_limit_bytes=VMEM_LIMIT),
        name="mixer",
    )(x, x, mod3, mod3, *prm, wg, wu, wd)


def _plan_kernel(pos_ref, lo_ref, hi_ref, inv_ref):
    def mark_unused(p, carry):
        inv_ref[p] = jnp.int32(-1)
        return carry

    for g in range(N_GROUPS + 1):
        lax.fori_loop(lo_ref[g], hi_ref[g], mark_unused, 0)

    def place(t, j):
        inv_ref[pos_ref[t]] = t

    _for_each_token(pos_ref.shape[0], place)


def _plan(pos, lo, hi, n_slots):
    return pl.pallas_call(
        _plan_kernel,
        out_shape=jax.ShapeDtypeStruct((n_slots,), jnp.int32),
        grid_spec=pltpu.PrefetchScalarGridSpec(
            num_scalar_prefetch=3,
            grid=(1,),
            in_specs=[],
            out_specs=pl.BlockSpec(memory_space=pltpu.SMEM)),
        compiler_params=pltpu.CompilerParams(dimension_semantics=("arbitrary",)),
        name="plan",
    )(pos, lo, hi)


def _experts_kernel(bg_ref, nv_ref, inv_ref, hx_ref, wg_ref, wu_ref, wd_ref, yt_ref,
                    xin, yout, xb, rt, xb_nxt, rt_nxt, hid, gsem, ssem, *, n_tokens):
    i = pl.program_id(0)
    nv = nv_ref[0]

    def gather(blk, r, buf):
        tok = jnp.maximum(inv_ref[blk * TB + r], 0)
        return pltpu.make_async_copy(hx_ref.at[_token_tile(tok, TOK_ROWS), :],
                                     xin.at[buf, _token_tile(r, TOK_ROWS), :], gsem.at[buf])

    def scatter(blk, r, buf):
        tok = inv_ref[jnp.maximum(blk, 0) * TB + r]
        dst = jnp.where((tok < 0) | (blk < 0), n_tokens + buf * TB + r, tok)
        return pltpu.make_async_copy(yout.at[buf, _token_tile(r, SUB), :],
                                     yt_ref.at[_token_tile(dst, SUB), :], ssem.at[buf])

    def wait_gathers(buf):
        pltpu.make_async_copy(hx_ref.at[pl.ds(0, TB * TOK_ROWS), :], xin.at[buf],
                              gsem.at[buf]).wait()

    def wait_scatters(buf):
        pltpu.make_async_copy(yout.at[buf], yt_ref.at[pl.ds(0, TB * SUB), :],
                              ssem.at[buf]).wait()

    def unpack(buf, x_dst, r_dst):
        xs = xin.at[buf]
        for c in range(ACT_ROWS):
            x_dst[:, c * LANES:(c + 1) * LANES] = xs[_tile_rows(c, TB, TOK_ROWS), :].astype(BF16)
        r_dst[...] = xs[_tile_rows(ROUTE_ROW, TB, TOK_ROWS), :]

    gcur = lax.rem(i, GRING)
    scur = lax.rem(i, SRING)
    sprv = lax.rem(i + SRING - 1, SRING)
    snxt = lax.rem(i + 1, SRING)

    @pl.when(i == 0)
    def _():
        xin[...] = jnp.zeros(xin.shape, F32)
        yout[...] = jnp.zeros(yout.shape, F32)
        for b in range(SRING):
            spare = pl.ds((n_tokens + b * TB) * SUB, TB * SUB)
            clear = pltpu.make_async_copy(yout.at[b], yt_ref.at[spare, :], ssem.at[b])
            clear.start()
            clear.wait()
        for b in range(GRING - 1):
            blk = jnp.minimum(b, nv - 1)
            _for_each_token(TB, lambda r, j: gather(blk, r, b).start(priority=j % 2))
        wait_gathers(0)
        unpack(0, xb, rt)

    @pl.when(i < nv)
    def _():
        wait_gathers(lax.rem(i + 1, GRING))

    @pl.when((i < nv) & (i >= SRING - 1))
    def _():
        wait_scatters(scur)

    @pl.when(i < nv)
    def _():
        ahead = jnp.minimum(i + GRING - 1, nv - 1)
        gfar = lax.rem(i + GRING - 1, GRING)
        unpack(lax.rem(i + 1, GRING), xb_nxt, rt_nxt)
        x = xb[...]
        route = rt[...]
        per_expert = TB // EXPERTS_PER_GROUP
        for e in range(EXPERTS_PER_GROUP):
            g = _dot(x, wg_ref[e])
            u = _dot(x, wu_ref[e])
            act = (g * _sigmoid(g)) * u * route[:, e:e + 1]
            hid[:, e * D_EXPERT:(e + 1) * D_EXPERT] = act.astype(BF16)
            for r in range(e * per_expert, (e + 1) * per_expert):
                gather(ahead, r, gfar).start(priority=r % 2)
                scatter(i - 1, r, sprv).start(priority=(r + 1) % 2)
        y = _dot(hid[...], wd_ref[...])
        ys = yout.at[scur]
        for c in range(SUB):
            ys[_tile_rows(c, TB, SUB), :] = y[:, c * LANES:(c + 1) * LANES]
        xb[...] = xb_nxt[...]
        rt[...] = rt_nxt[...]

    @pl.when(i == nv - 1)
    def _():
        _for_each_token(TB, lambda r, j: scatter(i, r, scur).start(priority=j % 2))
        wait_scatters(sprv)
        wait_scatters(scur)
        wait_gathers(lax.rem(i + 2, GRING))
        wait_gathers(lax.rem(i + 3, GRING))

    @pl.when((i == nv - 1) & (i >= 1))
    def _():
        wait_scatters(snxt)


def _experts(blk_grp, n_valid, inv, hx, wg, wu, wd, n_blocks):
    n_tokens = hx.shape[0] // TOK_ROWS

    def w_map(i, bg, nv, inv_):
        return (bg[i], 0, 0)

    def wd_map(i, bg, nv, inv_):
        return (bg[i], 0)

    return pl.pallas_call(
        functools.partial(_experts_kernel, n_tokens=n_tokens),
        out_shape=jax.ShapeDtypeStruct(((n_tokens + SRING * TB) * SUB, LANES), F32),
        grid_spec=pltpu.PrefetchScalarGridSpec(
            num_scalar_prefetch=3,
            grid=(n_blocks,),
            in_specs=[
                pl.BlockSpec(memory_space=pl.ANY),
                pl.BlockSpec((EXPERTS_PER_GROUP, D_MODEL, D_EXPERT), w_map),
                pl.BlockSpec((EXPERTS_PER_GROUP, D_MODEL, D_EXPERT), w_map),
                pl.BlockSpec((EXPERTS_PER_GROUP * D_EXPERT, D_MODEL), wd_map),
            ],
            out_specs=pl.BlockSpec(memory_space=pl.ANY),
            scratch_shapes=[
                pltpu.VMEM((GRING, TB * TOK_ROWS, LANES), F32),
                pltpu.VMEM((SRING, TB * SUB, LANES), F32),
                pltpu.VMEM((TB, D_MODEL), BF16),
                pltpu.VMEM((TB, LANES), F32),
                pltpu.VMEM((TB, D_MODEL), BF16),
                pltpu.VMEM((TB, LANES), F32),
                pltpu.VMEM((TB, EXPERTS_PER_GROUP * D_EXPERT), BF16),
                pltpu.SemaphoreType.DMA((GRING,)),
                pltpu.SemaphoreType.DMA((SRING,)),
            ]),
        compiler_params=pltpu.CompilerParams(
            dimension_semantics=("arbitrary",),
            vmem_limit_bytes=VMEM_LIMIT),
        name="experts",
    )(blk_grp, n_valid, inv, hx, wg, wu, wd)


def _combine_kernel(x1_ref, mod_ref, gfin_ref, yt_ref, o_ref):
    gt2 = mod_ref[0][5:6]
    x2 = []
    ssq = jnp.zeros((TC, 1), F32)
    for c in range(SUB):
        cols = slice(c * LANES, (c + 1) * LANES)
        v = x1_ref[:, cols] + gt2[:, cols] * yt_ref[_tile_rows(c, TC, SUB), :]
        ssq = ssq + jnp.sum(v * v, axis=-1, keepdims=True)
        x2.append(v)
    scale = lax.rsqrt(ssq / D_MODEL + EPS)
    for c in range(SUB):
        cols = slice(c * LANES, (c + 1) * LANES)
        o_ref[:, cols] = x2[c] * scale * gfin_ref[:, cols]


def _combine(x1, mod3, g_final, yt, seq):
    t = x1.shape[0]
    per_batch = seq // TC
    return pl.pallas_call(
        _combine_kernel,
        out_shape=jax.ShapeDtypeStruct((t, D_MODEL), F32),
        grid=(t // TC,),
        in_specs=[
            pl.BlockSpec((TC, D_MODEL), lambda i: (i, 0)),
            pl.BlockSpec((1, SUB, D_MODEL), lambda i: (i // per_batch, 0, 0)),
            pl.BlockSpec((1, D_MODEL), lambda i: (0, 0)),
            pl.BlockSpec((TC * SUB, LANES), lambda i: (i, 0)),
        ],
        out_specs=pl.BlockSpec((TC, D_MODEL), lambda i: (i, 0)),
        compiler_params=pltpu.CompilerParams(dimension_semantics=("arbitrary",)),
        name="combine",
    )(x1, mod3, g_final, yt)


def _split_bf16(w):
    hi = w.astype(BF16)
    return hi, (w - hi.astype(F32)).astype(BF16)


def kernel(x, c, w_ada, b_ada, g_mix, w_in, conv_w, conv_b, conv_ln_g, conv_ln_b, sgu_ln_g,
           sgu_ln_b, sgu_w, sgu_b, w_out, g_ffn, w_coarse, b_coarse, w_fine, b_fine, w_gate,
           w_up, w_down, g_final):
    bsz, seq, d = x.shape
    t = bsz * seq
    n_blocks = t // TB + N_GROUPS

    mod = _ada(c.T, w_ada[0], b_ada[0][None, :])
    mod3 = jnp.pad(mod.reshape(bsz, 6, d), ((0, 0), (0, 2), (0, 0)))

    tril = jnp.tril(jnp.ones((CHUNK, CHUNK), F32))
    ws = (sgu_w[0] * tril).astype(BF16)
    w2 = jnp.concatenate([ws[0::2], ws[1::2]], axis=2)
    bfull = jnp.repeat(sgu_b[0].T, HEAD_DIM, axis=1)
    ch = jnp.arange(CONV_WIDTH) // HEAD_DIM
    a64 = ((ch[:, None] == ch[None, :]).astype(F32) / HEAD_DIM).astype(BF16)
    convw = jnp.pad(conv_w[0], ((0, HALO - CONV_KSIZE), (0, 0)))
    wr = jnp.zeros((ROUTER_ROWS, d), F32)
    wr = wr.at[:N_GROUPS].set(w_coarse[0].T)
    wr = wr.at[SUB:SUB * (1 + N_GROUPS)].set(
        jnp.transpose(w_fine[0], (0, 2, 1)).reshape(N_GROUPS * EXPERTS_PER_GROUP, d))
    br = jnp.zeros((ROUTER_ROWS, 1), F32)
    br = br.at[:N_GROUPS, 0].set(b_coarse[0])
    br = br.at[SUB:SUB * (1 + N_GROUPS), 0].set(b_fine[0].reshape(-1))
    wrh, wrl = _split_bf16(wr)
    ri = jnp.arange(TM)
    utri = (ri[:, None] < ri[None, :]).astype(BF16)
    prm = (g_mix, w_in[0].astype(BF16), convw, conv_b, conv_ln_g[0].reshape(1, -1),
           conv_ln_b[0].reshape(1, -1), sgu_ln_g[0].reshape(1, -1), sgu_ln_b[0].reshape(1, -1),
           a64, w2, bfull, w_out[0].astype(BF16), g_ffn, wrh, wrl, br, utri)

    x1, hx, meta, cnt, wg, wu, wd = _mixer(x, mod3, prm, w_gate[0], w_up[0], w_down[0])

    counts = cnt[:N_GROUPS, 0].astype(jnp.int32)
    nblk = (counts + TB - 1) // TB
    blk_end = jnp.cumsum(nblk)
    blk_start = blk_end - nblk
    grp = meta[META_GRP].astype(jnp.int32)
    rank = meta[META_RANK].astype(jnp.int32)
    pos = blk_start[grp] * TB + rank
    n_valid = blk_end[N_GROUPS - 1:]
    unused_lo = jnp.concatenate([blk_start * TB + counts, n_valid * TB])
    unused_hi = jnp.concatenate([blk_end * TB, jnp.full((1,), n_blocks * TB, jnp.int32)])
    blk_ids = jnp.minimum(jnp.arange(n_blocks, dtype=jnp.int32), n_valid[0] - 1)
    blk_grp = jnp.sum((blk_ids[:, None] >= blk_end[None, :]).astype(jnp.int32), axis=1)

    inv = _plan(pos, unused_lo, unused_hi, n_blocks * TB)
    yt = _experts(blk_grp, n_valid, inv, hx, wg, wu,
                  wd.reshape(N_GROUPS * EXPERTS_PER_GROUP * D_EXPERT, d), n_blocks)
    out = _combine(x1.reshape(t, d), mod3, g_final[None, :], yt, seq)
    return out.reshape(bsz, seq, d)
```

```python
import functools

import jax
import jax.numpy as jnp
from jax import lax
from jax.experimental import pallas as pl
from jax.experimental.pallas import tpu as pltpu

F32 = jnp.float32
BF16 = jnp.bfloat16

D_MODEL = 1024
CONV_WIDTH = 512
CONV_KSIZE = 31
GMLP_WIDTH = 512
HEAD_DIM = 64
CHUNK = 128
N_GROUPS = 4
EXPERTS_PER_GROUP = 8
D_EXPERT = 256
EPS = 1e-6
assert EXPERTS_PER_GROUP == 8

LANES = 128
SUB = 8
TM = 512
HALO = 32
TB = 256
TC = 512
TOK_ROWS = D_MODEL // LANES
assert TOK_ROWS == SUB
META_GRP = 0
META_RANK = 1
ROUTER_ROWS = SUB * (1 + N_GROUPS) + SUB
DMA_UNROLL = 8
GRING = 4
SRING = 3
VMEM_LIMIT = 56 * 1024 * 1024


def _dot(a, b):
    return jnp.dot(a, b, preferred_element_type=F32)


def _sigmoid(x):
    return 0.5 * jnp.tanh(0.5 * x) + 0.5


def _gelu_tanh(x):
    c = 0.7978845608028654
    return 0.5 * x * (1.0 + jnp.tanh(c * (x + 0.044715 * (x * x * x))))


def _tile_rows(c, n, per_token):
    return pl.ds(c, n, stride=per_token)


def _token_tile(t, per_token):
    if isinstance(t, int):
        return pl.ds(t * per_token, per_token)
    return pl.ds(pl.multiple_of(t * per_token, per_token), per_token)


def _for_each_token(n, fn):
    def group(gi, carry):
        for j in range(DMA_UNROLL):
            fn(gi * DMA_UNROLL + j, j)
        return carry
    lax.fori_loop(0, n // DMA_UNROLL, group, 0)


def _ada_kernel(ct_ref, w_ref, b_ref, o_ref):
    ct = ct_ref[...]
    ca = ct * _sigmoid(ct)
    w = w_ref[...]
    for b in range(ct.shape[1]):
        o_ref[b:b + 1, :] = jnp.sum(w * ca[:, b:b + 1], axis=0, keepdims=True) + b_ref[...]


def _ada(c_t, w_ada, b_ada):
    n = w_ada.shape[1]
    bsz = c_t.shape[1]
    tn = 1024
    return pl.pallas_call(
        _ada_kernel,
        out_shape=jax.ShapeDtypeStruct((bsz, n), F32),
        grid=(n // tn,),
        in_specs=[
            pl.BlockSpec((D_MODEL, bsz), lambda j: (0, 0)),
            pl.BlockSpec((D_MODEL, tn), lambda j: (0, j)),
            pl.BlockSpec((1, tn), lambda j: (0, j)),
        ],
        out_specs=pl.BlockSpec((bsz, tn), lambda j: (0, j)),
        name="ada",
    )(c_t, w_ada, b_ada)


def _group_layer_norm(v, a64, g, b):
    mu = _dot(v.astype(BF16), a64)
    d = v - mu
    var = _dot((d * d).astype(BF16), a64)
    return d * lax.rsqrt(var + EPS) * g + b


def _causal_conv_chunk(ybuf, cw, r0):
    out = None
    for r in range(SUB):
        part = None
        for q in range((CONV_KSIZE - 1 - r) // SUB + 1):
            k = CONV_KSIZE - 1 - (SUB * q + r)
            seg = ybuf[pl.ds(r0 + HALO - SUB - SUB * q, CHUNK + SUB), :]
            term = seg * cw[k:k + 1, :]
            part = term if part is None else part + term
        shifted = part[SUB - r:SUB - r + CHUNK, :]
        out = shifted if out is None else out + shifted
    return out


def _mixer_kernel(x_ref, xp_ref, mod_ref, modp_ref, gmix_ref, win_ref, convw_ref, convb_ref,
                  clng_ref, clnb_ref, slng_ref, slnb_ref, a64_ref, w2_ref, bfull_ref, wout_ref,
                  gffn_ref, wrh_ref, wrl_ref, br_ref, utri_ref, wg_ref, wu_ref, wd_ref,
                  x1_ref, hx_ref, route_ref, meta_ref, cnt_ref, wgb_ref, wub_ref, wdb_ref,
                  ybuf, zu_s, vn_s, cat_w, cat_r, h2_s, carry, *, tiles_per_seq):
    s = pl.program_id(0)
    n_tiles = pl.num_programs(0) - 1

    @pl.when(lax.rem(jnp.minimum(s, n_tiles - 1), tiles_per_seq) == 0)
    def _():
        ybuf[0:HALO, :] = jnp.zeros((HALO, CONV_WIDTH), F32)

    @pl.when(s == 0)
    def _():
        carry[...] = jnp.zeros((SUB, 1), F32)
        cat_r[...] = jnp.zeros((TM, D_MODEL), BF16)

    counted = jnp.where(s > 0, 1.0, 0.0)

    wgb_ref[...] = wg_ref[...].astype(BF16)
    wub_ref[...] = wu_ref[...].astype(BF16)
    wdb_ref[...] = wd_ref[...].astype(BF16)

    x = x_ref[0]
    mod = mod_ref[0]
    sh1, sc1 = mod[0:1], mod[1:2]

    ms = jnp.mean(x * x, axis=-1, keepdims=True)
    h = x * lax.rsqrt(ms + EPS) * (gmix_ref[...] * (1.0 + sc1)) + sh1
    hb = h.astype(BF16)
    a64 = a64_ref[...]

    ga = _dot(hb, win_ref[:, 0:CONV_WIDTH])
    gg = _dot(hb, win_ref[:, CONV_WIDTH:2 * CONV_WIDTH])
    ybuf[HALO:HALO + TM, :] = ga * _sigmoid(gg)
    off = 2 * CONV_WIDTH
    zu_s[...] = _gelu_tanh(_dot(hb, win_ref[:, off:off + GMLP_WIDTH]))
    zv = _gelu_tanh(_dot(hb, win_ref[:, off + GMLP_WIDTH:off + 2 * GMLP_WIDTH]))
    vn_s[...] = _group_layer_norm(zv, a64, slng_ref[...], slnb_ref[...])

    cw = convw_ref[...]
    low = lax.broadcasted_iota(jnp.int32, (CHUNK, LANES), 1) < HEAD_DIM

    def back_residual():
        gt1 = modp_ref[0][2:3]
        x1_ref[0] = xp_ref[0] + gt1 * _dot(cat_r[...], wout_ref[...])

    def back_router():
        modp = modp_ref[0]
        sh2, sc2 = modp[3:4], modp[4:5]
        x1 = x1_ref[0]
        ms2 = jnp.mean(x1 * x1, axis=-1, keepdims=True)
        h2 = x1 * lax.rsqrt(ms2 + EPS) * (gffn_ref[...] * (1.0 + sc2)) + sh2
        h2_s[...] = h2
        hh = h2.astype(BF16)
        hl = (h2 - hh.astype(F32)).astype(BF16)

        def nt_dot(w, a):
            return lax.dot_general(w, a, (((1,), (1,)), ((), ())), preferred_element_type=F32)

        lt = (nt_dot(wrh_ref[...], hh) + nt_dot(wrh_ref[...], hl) + nt_dot(wrl_ref[...], hh)
              + br_ref[...])
        row = lax.broadcasted_iota(jnp.int32, (SUB, TM), 0).astype(F32)
        neg = jnp.float32(-jnp.inf)
        far = jnp.float32(SUB)
        cm = row < N_GROUPS
        lc = lt[0:SUB]
        m = jnp.max(jnp.where(cm, lc, neg), axis=0, keepdims=True)
        ec = jnp.exp(jnp.where(cm, lc - m, neg))
        p_grp = 1.0 / jnp.sum(ec, axis=0, keepdims=True)
        grp = jnp.min(jnp.where(cm & (lc == m), row, far), axis=0, keepdims=True)

        lf = lt[SUB:2 * SUB]
        for g in range(1, N_GROUPS):
            lf = jnp.where(grp == g, lt[(g + 1) * SUB:(g + 2) * SUB], lf)
        mf = jnp.max(lf, axis=0, keepdims=True)
        ef = jnp.exp(lf - mf)
        fp = ef / jnp.sum(ef, axis=0, keepdims=True)
        v1 = jnp.max(fp, axis=0, keepdims=True)
        i1 = jnp.min(jnp.where(fp == v1, row, far), axis=0, keepdims=True)
        rest = row != i1
        v2 = jnp.max(jnp.where(rest, fp, -1.0), axis=0, keepdims=True)
        i2 = jnp.min(jnp.where(rest & (fp == v2), row, far), axis=0, keepdims=True)
        den = v1 + v2
        gate = p_grp * jnp.where(row == i1, v1 / den, jnp.where(row == i2, v2 / den, 0.0))

        onehot = (row == grp).astype(F32) * counted
        before = _dot(onehot.astype(BF16), utri_ref[...]) + carry[...]
        rank = jnp.sum(before * onehot, axis=0, keepdims=True)
        carry[...] = carry[...] + jnp.sum(onehot, axis=1, keepdims=True)
        cnt_ref[...] = jnp.broadcast_to(carry[...], (SUB, LANES))
        meta_ref[...] = jnp.where(row == META_GRP, grp, jnp.where(row == META_RANK, rank, 0.0))

        route_ref[...] = jnp.concatenate([gate, jnp.zeros((LANES - SUB, TM), F32)], axis=0).T

    def back_tile_low():
        for c in range(TOK_ROWS // 2):
            hx_ref[_tile_rows(c, TM, TOK_ROWS), :] = h2_s[:, c * LANES:(c + 1) * LANES]

    def back_tile_high():
        for c in range(TOK_ROWS // 2, TOK_ROWS):
            hx_ref[_tile_rows(c, TM, TOK_ROWS), :] = h2_s[:, c * LANES:(c + 1) * LANES]

    back = (back_residual, back_router, back_tile_low, back_tile_high)
    assert len(back) == TM // CHUNK

    for c in range(TM // CHUNK):
        r0 = c * CHUNK
        rows = pl.ds(r0, CHUNK)
        cv = _causal_conv_chunk(ybuf, cw, r0) + convb_ref[...]
        cv = _group_layer_norm(cv, a64, clng_ref[...], clnb_ref[...])
        cat_w[rows, 0:CONV_WIDTH] = (cv * _sigmoid(cv)).astype(BF16)
        for j in range(GMLP_WIDTH // LANES):
            cols = slice(j * LANES, (j + 1) * LANES)
            vj = vn_s[rows, cols]
            rhs = jnp.concatenate([jnp.where(low, vj, 0.0).astype(BF16),
                                   jnp.where(low, 0.0, vj).astype(BF16)], axis=0)
            mixed = _dot(w2_ref[j], rhs) + bfull_ref[:, cols]
            cat_w[rows, CONV_WIDTH + j * LANES:CONV_WIDTH + (j + 1) * LANES] = (
                zu_s[rows, cols] * mixed).astype(BF16)
        back[c]()
    ybuf[0:HALO, :] = ybuf[TM:TM + HALO, :]
    cat_r[...] = cat_w[...]


def _mixer(x, mod3, prm, wg, wu, wd):
    bsz, seq, _ = x.shape
    nt = seq // TM
    n_tiles = bsz * nt
    n_exp = wg.shape[0]
    assert n_exp <= n_tiles + 1
    w_blk = lambda w: pl.BlockSpec((1,) + w.shape[1:], lambda s: (jnp.minimum(s, n_exp - 1), 0, 0))

    def cur(s):
        return jnp.minimum(s, n_tiles - 1)

    def prev(s):
        return jnp.maximum(s - 1, 0)

    const = lambda shape: pl.BlockSpec(shape, lambda s: (0,) * len(shape))
    in_specs = [
        pl.BlockSpec((1, TM, D_MODEL), lambda s: (cur(s) // nt, cur(s) % nt, 0)),
        pl.BlockSpec((1, TM, D_MODEL), lambda s: (prev(s) // nt, prev(s) % nt, 0)),
        pl.BlockSpec((1, SUB, D_MODEL), lambda s: (cur(s) // nt, 0, 0)),
        pl.BlockSpec((1, SUB, D_MODEL), lambda s: (prev(s) // nt, 0, 0)),
        const((1, D_MODEL)),
        const((D_MODEL, 4 * CONV_WIDTH)),
        const((HALO, CONV_WIDTH)),
        const((1, CONV_WIDTH)), const((1, CONV_WIDTH)), const((1, CONV_WIDTH)),
        const((1, GMLP_WIDTH)), const((1, GMLP_WIDTH)),
        const((CONV_WIDTH, CONV_WIDTH)),
        const((GMLP_WIDTH // LANES, CHUNK, 2 * CHUNK)),
        const((CHUNK, GMLP_WIDTH)),
        const((D_MODEL, D_MODEL)),
        const((1, D_MODEL)),
        const((ROUTER_ROWS, D_MODEL)), const((ROUTER_ROWS, D_MODEL)), const((ROUTER_ROWS, 1)),
        const((TM, TM)),
        w_blk(wg), w_blk(wu), w_blk(wd),
    ]
    out_shape = (
        jax.ShapeDtypeStruct((bsz, seq, D_MODEL), F32),
        jax.ShapeDtypeStruct((bsz * seq * TOK_ROWS, LANES), F32),
        jax.ShapeDtypeStruct((bsz * seq, LANES), F32),
        jax.ShapeDtypeStruct((SUB, bsz * seq), F32),
        jax.ShapeDtypeStruct((SUB, LANES), F32),
        jax.ShapeDtypeStruct(wg.shape, BF16),
        jax.ShapeDtypeStruct(wu.shape, BF16),
        jax.ShapeDtypeStruct(wd.shape, BF16),
    )
    out_specs = (
        pl.BlockSpec((1, TM, D_MODEL), lambda s: (prev(s) // nt, prev(s) % nt, 0)),
        pl.BlockSpec((TM * TOK_ROWS, LANES), lambda s: (prev(s), 0)),
        pl.BlockSpec((TM, LANES), lambda s: (prev(s), 0)),
        pl.BlockSpec((SUB, TM), lambda s: (0, prev(s))),
        pl.BlockSpec((SUB, LANES), lambda s: (0, 0)),
        w_blk(wg), w_blk(wu), w_blk(wd),
    )
    return pl.pallas_call(
        functools.partial(_mixer_kernel, tiles_per_seq=nt),
        out_shape=out_shape,
        grid=(n_tiles + 1,),
        in_specs=in_specs,
        out_specs=out_specs,
        scratch_shapes=[
            pltpu.VMEM((TM + HALO, CONV_WIDTH), F32),
            pltpu.VMEM((TM, GMLP_WIDTH), F32),
            pltpu.VMEM((TM, GMLP_WIDTH), F32),
            pltpu.VMEM((TM, D_MODEL), BF16),
            pltpu.VMEM((TM, D_MODEL), BF16),
            pltpu.VMEM((TM, D_MODEL), F32),
            pltpu.VMEM((SUB, 1), F32),
        ],
        compiler_params=pltpu.CompilerParams(
            dimension_semantics=("arbitrary",),
            vmem_limit_bytes=VMEM_LIMIT),
        name="mixer",
    )(x, x, mod3, mod3, *prm, wg, wu, wd)


def _plan_kernel(pos_ref, lo_ref, hi_ref, inv_ref):
    def mark_unused(p, carry):
        inv_ref[p] = jnp.int32(-1)
        return carry

    for g in range(N_GROUPS + 1):
        lax.fori_loop(lo_ref[g], hi_ref[g], mark_unused, 0)

    def place(t, j):
        inv_ref[pos_ref[t]] = t

    _for_each_token(pos_ref.shape[0], place)


def _plan(pos, lo, hi, n_slots):
    return pl.pallas_call(
        _plan_kernel,
        out_shape=jax.ShapeDtypeStruct((n_slots,), jnp.int32),
        grid_spec=pltpu.PrefetchScalarGridSpec(
            num_scalar_prefetch=3,
            grid=(1,),
            in_specs=[],
            out_specs=pl.BlockSpec(memory_space=pltpu.SMEM)),
        compiler_params=pltpu.CompilerParams(dimension_semantics=("arbitrary",)),
        name="plan",
    )(pos, lo, hi)


def _experts_kernel(bg_ref, nv_ref, inv_ref, hx_ref, rtab_ref, wg_ref, wu_ref, wd_ref, yt_ref,
                    xin, rin, yout, xb, rt, xb_nxt, rt_nxt, hid, gsem, ssem, *, n_tokens):
    i = pl.program_id(0)
    nv = nv_ref[0]

    def gather(blk, r, buf):
        tok = jnp.maximum(inv_ref[blk * TB + r], 0)
        rin[buf, pl.ds(r, 1), :] = rtab_ref[pl.ds(tok, 1), :]
        return pltpu.make_async_copy(hx_ref.at[_token_tile(tok, TOK_ROWS), :],
                                     xin.at[buf, _token_tile(r, TOK_ROWS), :], gsem.at[buf])

    def scatter(blk, r, buf):
        tok = inv_ref[jnp.maximum(blk, 0) * TB + r]
        dst = jnp.where((tok < 0) | (blk < 0), n_tokens + buf * TB + r, tok)
        return pltpu.make_async_copy(yout.at[buf, _token_tile(r, SUB), :],
                                     yt_ref.at[_token_tile(dst, SUB), :], ssem.at[buf])

    def wait_gathers(buf):
        pltpu.make_async_copy(hx_ref.at[pl.ds(0, TB * TOK_ROWS), :], xin.at[buf],
                              gsem.at[buf]).wait()

    def wait_scatters(buf):
        pltpu.make_async_copy(yout.at[buf], yt_ref.at[pl.ds(0, TB * SUB), :],
                              ssem.at[buf]).wait()

    def unpack(buf, x_dst, r_dst):
        xs = xin.at[buf]
        for c in range(TOK_ROWS):
            x_dst[:, c * LANES:(c + 1) * LANES] = xs[_tile_rows(c, TB, TOK_ROWS), :].astype(BF16)
        r_dst[...] = rin[buf]

    gcur = lax.rem(i, GRING)
    scur = lax.rem(i, SRING)
    sprv = lax.rem(i + SRING - 1, SRING)
    snxt = lax.rem(i + 1, SRING)

    @pl.when(i == 0)
    def _():
        xin[...] = jnp.zeros(xin.shape, F32)
        rin[...] = jnp.zeros(rin.shape, F32)
        yout[...] = jnp.zeros(yout.shape, F32)
        for b in range(SRING):
            spare = pl.ds((n_tokens + b * TB) * SUB, TB * SUB)
            clear = pltpu.make_async_copy(yout.at[b], yt_ref.at[spare, :], ssem.at[b])
            clear.start()
            clear.wait()
        for b in range(GRING - 1):
            blk = jnp.minimum(b, nv - 1)
            _for_each_token(TB, lambda r, j: gather(blk, r, b).start(priority=j % 2))
        wait_gathers(0)
        unpack(0, xb, rt)

    @pl.when(i < nv)
    def _():
        wait_gathers(lax.rem(i + 1, GRING))

    @pl.when((i < nv) & (i >= SRING - 1))
    def _():
        wait_scatters(scur)

    @pl.when(i < nv)
    def _():
        ahead = jnp.minimum(i + GRING - 1, nv - 1)
        gfar = lax.rem(i + GRING - 1, GRING)
        unpack(lax.rem(i + 1, GRING), xb_nxt, rt_nxt)
        x = xb[...]
        route = rt[...]
        per_expert = TB // EXPERTS_PER_GROUP
        for e in range(EXPERTS_PER_GROUP):
            g = _dot(x, wg_ref[e])
            u = _dot(x, wu_ref[e])
            act = (g * _sigmoid(g)) * u * route[:, e:e + 1]
            hid[:, e * D_EXPERT:(e + 1) * D_EXPERT] = act.astype(BF16)
            for r in range(e * per_expert, (e + 1) * per_expert):
                gather(ahead, r, gfar).start(priority=r % 2)
                scatter(i - 1, r, sprv).start(priority=(r + 1) % 2)
        y = _dot(hid[...], wd_ref[...])
        ys = yout.at[scur]
        for c in range(SUB):
            ys[_tile_rows(c, TB, SUB), :] = y[:, c * LANES:(c + 1) * LANES]
        xb[...] = xb_nxt[...]
        rt[...] = rt_nxt[...]

    @pl.when(i == nv - 1)
    def _():
        _for_each_token(TB, lambda r, j: scatter(i, r, scur).start(priority=j % 2))
        wait_scatters(sprv)
        wait_scatters(scur)
        wait_gathers(lax.rem(i + 2, GRING))
        wait_gathers(lax.rem(i + 3, GRING))

    @pl.when((i == nv - 1) & (i >= 1))
    def _():
        wait_scatters(snxt)


def _experts(blk_grp, n_valid, inv, hx, rtab, wg, wu, wd, n_blocks):
    n_tokens = hx.shape[0] // TOK_ROWS

    def w_map(i, bg, nv, inv_):
        return (bg[i], 0, 0)

    def wd_map(i, bg, nv, inv_):
        return (bg[i], 0)

    return pl.pallas_call(
        functools.partial(_experts_kernel, n_tokens=n_tokens),
        out_shape=jax.ShapeDtypeStruct(((n_tokens + SRING * TB) * SUB, LANES), F32),
        grid_spec=pltpu.PrefetchScalarGridSpec(
            num_scalar_prefetch=3,
            grid=(n_blocks,),
            in_specs=[
                pl.BlockSpec(memory_space=pl.ANY),
                pl.BlockSpec(rtab.shape, lambda i, bg, nv, inv_: (0, 0)),
                pl.BlockSpec((EXPERTS_PER_GROUP, D_MODEL, D_EXPERT), w_map),
                pl.BlockSpec((EXPERTS_PER_GROUP, D_MODEL, D_EXPERT), w_map),
                pl.BlockSpec((EXPERTS_PER_GROUP * D_EXPERT, D_MODEL), wd_map),
            ],
            out_specs=pl.BlockSpec(memory_space=pl.ANY),
            scratch_shapes=[
                pltpu.VMEM((GRING, TB * TOK_ROWS, LANES), F32),
                pltpu.VMEM((GRING, TB, LANES), F32),
                pltpu.VMEM((SRING, TB * SUB, LANES), F32),
                pltpu.VMEM((TB, D_MODEL), BF16),
                pltpu.VMEM((TB, LANES), F32),
                pltpu.VMEM((TB, D_MODEL), BF16),
                pltpu.VMEM((TB, LANES), F32),
                pltpu.VMEM((TB, EXPERTS_PER_GROUP * D_EXPERT), BF16),
                pltpu.SemaphoreType.DMA((GRING,)),
                pltpu.SemaphoreType.DMA((SRING,)),
            ]),
        compiler_params=pltpu.CompilerParams(
            dimension_semantics=("arbitrary",),
            vmem_limit_bytes=VMEM_LIMIT),
        name="experts",
    )(blk_grp, n_valid, inv, hx, rtab, wg, wu, wd)


def _combine_kernel(x1_ref, mod_ref, gfin_ref, yt_ref, o_ref):
    gt2 = mod_ref[0][5:6]
    x2 = []
    ssq = jnp.zeros((TC, 1), F32)
    for c in range(SUB):
        cols = slice(c * LANES, (c + 1) * LANES)
        v = x1_ref[:, cols] + gt2[:, cols] * yt_ref[_tile_rows(c, TC, SUB), :]
        ssq = ssq + jnp.sum(v * v, axis=-1, keepdims=True)
        x2.append(v)
    scale = lax.rsqrt(ssq / D_MODEL + EPS)
    for c in range(SUB):
        cols = slice(c * LANES, (c + 1) * LANES)
        o_ref[:, cols] = x2[c] * scale * gfin_ref[:, cols]


def _combine(x1, mod3, g_final, yt, seq):
    t = x1.shape[0]
    per_batch = seq // TC
    return pl.pallas_call(
        _combine_kernel,
        out_shape=jax.ShapeDtypeStruct((t, D_MODEL), F32),
        grid=(t // TC,),
        in_specs=[
            pl.BlockSpec((TC, D_MODEL), lambda i: (i, 0)),
            pl.BlockSpec((1, SUB, D_MODEL), lambda i: (i // per_batch, 0, 0)),
            pl.BlockSpec((1, D_MODEL), lambda i: (0, 0)),
            pl.BlockSpec((TC * SUB, LANES), lambda i: (i, 0)),
        ],
        out_specs=pl.BlockSpec((TC, D_MODEL), lambda i: (i, 0)),
        compiler_params=pltpu.CompilerParams(dimension_semantics=("arbitrary",)),
        name="combine",
    )(x1, mod3, g_final, yt)


def _split_bf16(w):
    hi = w.astype(BF16)
    return hi, (w - hi.astype(F32)).astype(BF16)


def kernel(x, c, w_ada, b_ada, g_mix, w_in, conv_w, conv_b, conv_ln_g, conv_ln_b, sgu_ln_g,
           sgu_ln_b, sgu_w, sgu_b, w_out, g_ffn, w_coarse, b_coarse, w_fine, b_fine, w_gate,
           w_up, w_down, g_final):
    bsz, seq, d = x.shape
    t = bsz * seq
    n_blocks = t // TB + N_GROUPS

    mod = _ada(c.T, w_ada[0], b_ada[0][None, :])
    mod3 = jnp.pad(mod.reshape(bsz, 6, d), ((0, 0), (0, 2), (0, 0)))

    tril = jnp.tril(jnp.ones((CHUNK, CHUNK), F32))
    ws = (sgu_w[0] * tril).astype(BF16)
    w2 = jnp.concatenate([ws[0::2], ws[1::2]], axis=2)
    bfull = jnp.repeat(sgu_b[0].T, HEAD_DIM, axis=1)
    ch = jnp.arange(CONV_WIDTH) // HEAD_DIM
    a64 = ((ch[:, None] == ch[None, :]).astype(F32) / HEAD_DIM).astype(BF16)
    convw = jnp.pad(conv_w[0], ((0, HALO - CONV_KSIZE), (0, 0)))
    wr = jnp.zeros((ROUTER_ROWS, d), F32)
    wr = wr.at[:N_GROUPS].set(w_coarse[0].T)
    wr = wr.at[SUB:SUB * (1 + N_GROUPS)].set(
        jnp.transpose(w_fine[0], (0, 2, 1)).reshape(N_GROUPS * EXPERTS_PER_GROUP, d))
    br = jnp.zeros((ROUTER_ROWS, 1), F32)
    br = br.at[:N_GROUPS, 0].set(b_coarse[0])
    br = br.at[SUB:SUB * (1 + N_GROUPS), 0].set(b_fine[0].reshape(-1))
    wrh, wrl = _split_bf16(wr)
    ri = jnp.arange(TM)
    utri = (ri[:, None] < ri[None, :]).astype(BF16)
    prm = (g_mix, w_in[0].astype(BF16), convw, conv_b, conv_ln_g[0].reshape(1, -1),
           conv_ln_b[0].reshape(1, -1), sgu_ln_g[0].reshape(1, -1), sgu_ln_b[0].reshape(1, -1),
           a64, w2, bfull, w_out[0].astype(BF16), g_ffn, wrh, wrl, br, utri)

    x1, hx, rtab, meta, cnt, wg, wu, wd = _mixer(x, mod3, prm, w_gate[0], w_up[0], w_down[0])

    counts = cnt[:N_GROUPS, 0].astype(jnp.int32)
    nblk = (counts + TB - 1) // TB
    blk_end = jnp.cumsum(nblk)
    blk_start = blk_end - nblk
    grp = meta[META_GRP].astype(jnp.int32)
    rank = meta[META_RANK].astype(jnp.int32)
    pos = blk_start[grp] * TB + rank
    n_valid = blk_end[N_GROUPS - 1:]
    unused_lo = jnp.concatenate([blk_start * TB + counts, n_valid * TB])
    unused_hi = jnp.concatenate([blk_end * TB, jnp.full((1,), n_blocks * TB, jnp.int32)])
    blk_ids = jnp.minimum(jnp.arange(n_blocks, dtype=jnp.int32), n_valid[0] - 1)
    blk_grp = jnp.sum((blk_ids[:, None] >= blk_end[None, :]).astype(jnp.int32), axis=1)

    inv = _plan(pos, unused_lo, unused_hi, n_blocks * TB)
    yt = _experts(blk_grp, n_valid, inv, hx, rtab, wg, wu,
                  wd.reshape(N_GROUPS * EXPERTS_PER_GROUP * D_EXPERT, d), n_blocks)
    out = _combine(x1.reshape(t, d), mod3, g_final[None, :], yt, seq)
    return out.reshape(bsz, seq, d)
```

```python
import functools

import jax
import jax.numpy as jnp
from jax import lax
from jax.experimental import pallas as pl
from jax.experimental.pallas import tpu as pltpu

F32 = jnp.float32
BF16 = jnp.bfloat16

D_MODEL = 1024
CONV_WIDTH = 512
CONV_KSIZE = 31
GMLP_WIDTH = 512
HEAD_DIM = 64
CHUNK = 128
N_GROUPS = 4
EXPERTS_PER_GROUP = 8
D_EXPERT = 256
EPS = 1e-6
assert EXPERTS_PER_GROUP == 8

LANES = 128
SUB = 8
TM = 512
HALO = 32
TB = 256
TC = 1024
TOK_ROWS = D_MODEL // LANES
assert TOK_ROWS == SUB
META_GRP = 0
META_RANK = 1
ROUTER_ROWS = SUB * (1 + N_GROUPS) + SUB
DMA_UNROLL = 8
GRING = 4
SRING = 3
VMEM_LIMIT = 56 * 1024 * 1024


def _dot(a, b):
    return jnp.dot(a, b, preferred_element_type=F32)


def _sigmoid(x):
    return 0.5 * jnp.tanh(0.5 * x) + 0.5


def _gelu_tanh(x):
    c = 0.7978845608028654
    return 0.5 * x * (1.0 + jnp.tanh(c * (x + 0.044715 * (x * x * x))))


def _tile_rows(c, n, per_token):
    return pl.ds(c, n, stride=per_token)


def _token_tile(t, per_token):
    if isinstance(t, int):
        return pl.ds(t * per_token, per_token)
    return pl.ds(pl.multiple_of(t * per_token, per_token), per_token)


def _for_each_token(n, fn):
    def group(gi, carry):
        for j in range(DMA_UNROLL):
            fn(gi * DMA_UNROLL + j, j)
        return carry
    lax.fori_loop(0, n // DMA_UNROLL, group, 0)


def _ada_kernel(ct_ref, w_ref, b_ref, o_ref):
    ct = ct_ref[...]
    ca = ct * _sigmoid(ct)
    w = w_ref[...]
    for b in range(ct.shape[1]):
        o_ref[b:b + 1, :] = jnp.sum(w * ca[:, b:b + 1], axis=0, keepdims=True) + b_ref[...]


def _ada(c_t, w_ada, b_ada):
    n = w_ada.shape[1]
    bsz = c_t.shape[1]
    tn = 1024
    return pl.pallas_call(
        _ada_kernel,
        out_shape=jax.ShapeDtypeStruct((bsz, n), F32),
        grid=(n // tn,),
        in_specs=[
            pl.BlockSpec((D_MODEL, bsz), lambda j: (0, 0)),
            pl.BlockSpec((D_MODEL, tn), lambda j: (0, j)),
            pl.BlockSpec((1, tn), lambda j: (0, j)),
        ],
        out_specs=pl.BlockSpec((bsz, tn), lambda j: (0, j)),
        name="ada",
    )(c_t, w_ada, b_ada)


def _group_layer_norm(v, a64, g, b):
    mu = _dot(v.astype(BF16), a64)
    d = v - mu
    var = _dot((d * d).astype(BF16), a64)
    return d * lax.rsqrt(var + EPS) * g + b


def _causal_conv_chunk(ybuf, cw, r0):
    out = None
    for r in range(SUB):
        part = None
        for q in range((CONV_KSIZE - 1 - r) // SUB + 1):
            k = CONV_KSIZE - 1 - (SUB * q + r)
            seg = ybuf[pl.ds(r0 + HALO - SUB - SUB * q, CHUNK + SUB), :]
            term = seg * cw[k:k + 1, :]
            part = term if part is None else part + term
        shifted = part[SUB - r:SUB - r + CHUNK, :]
        out = shifted if out is None else out + shifted
    return out


def _mixer_kernel(x_ref, xp_ref, mod_ref, modp_ref, gmix_ref, win_ref, convw_ref, convb_ref,
                  clng_ref, clnb_ref, slng_ref, slnb_ref, a64_ref, w2_ref, bfull_ref, wout_ref,
                  gffn_ref, wrh_ref, wrl_ref, br_ref, utri_ref, wg_ref, wu_ref, wd_ref,
                  x1_ref, hx_ref, route_ref, meta_ref, cnt_ref, wgb_ref, wub_ref, wdb_ref,
                  ybuf, zu_s, vn_s, cat_w, cat_r, h2_s, carry, *, tiles_per_seq):
    s = pl.program_id(0)
    n_tiles = pl.num_programs(0) - 1

    @pl.when(lax.rem(jnp.minimum(s, n_tiles - 1), tiles_per_seq) == 0)
    def _():
        ybuf[0:HALO, :] = jnp.zeros((HALO, CONV_WIDTH), F32)

    @pl.when(s == 0)
    def _():
        carry[...] = jnp.zeros((SUB, 1), F32)
        cat_r[...] = jnp.zeros((TM, D_MODEL), BF16)

    counted = jnp.where(s > 0, 1.0, 0.0)

    wgb_ref[...] = wg_ref[...].astype(BF16)
    wub_ref[...] = wu_ref[...].astype(BF16)
    wdb_ref[...] = wd_ref[...].astype(BF16)

    x = x_ref[0]
    mod = mod_ref[0]
    sh1, sc1 = mod[0:1], mod[1:2]

    ms = jnp.mean(x * x, axis=-1, keepdims=True)
    h = x * lax.rsqrt(ms + EPS) * (gmix_ref[...] * (1.0 + sc1)) + sh1
    hb = h.astype(BF16)
    a64 = a64_ref[...]

    ga = _dot(hb, win_ref[:, 0:CONV_WIDTH])
    gg = _dot(hb, win_ref[:, CONV_WIDTH:2 * CONV_WIDTH])
    ybuf[HALO:HALO + TM, :] = ga * _sigmoid(gg)
    off = 2 * CONV_WIDTH
    zu_s[...] = _gelu_tanh(_dot(hb, win_ref[:, off:off + GMLP_WIDTH]))
    zv = _gelu_tanh(_dot(hb, win_ref[:, off + GMLP_WIDTH:off + 2 * GMLP_WIDTH]))
    vn_s[...] = _group_layer_norm(zv, a64, slng_ref[...], slnb_ref[...])

    cw = convw_ref[...]
    low = lax.broadcasted_iota(jnp.int32, (CHUNK, LANES), 1) < HEAD_DIM

    def back_residual():
        gt1 = modp_ref[0][2:3]
        x1_ref[0] = xp_ref[0] + gt1 * _dot(cat_r[...], wout_ref[...])

    def back_router():
        modp = modp_ref[0]
        sh2, sc2 = modp[3:4], modp[4:5]
        x1 = x1_ref[0]
        ms2 = jnp.mean(x1 * x1, axis=-1, keepdims=True)
        h2 = x1 * lax.rsqrt(ms2 + EPS) * (gffn_ref[...] * (1.0 + sc2)) + sh2
        h2_s[...] = h2
        hh = h2.astype(BF16)
        hl = (h2 - hh.astype(F32)).astype(BF16)

        def nt_dot(w, a):
            return lax.dot_general(w, a, (((1,), (1,)), ((), ())), preferred_element_type=F32)

        lt = (nt_dot(wrh_ref[...], hh) + nt_dot(wrh_ref[...], hl) + nt_dot(wrl_ref[...], hh)
              + br_ref[...])
        row = lax.broadcasted_iota(jnp.int32, (SUB, TM), 0).astype(F32)
        neg = jnp.float32(-jnp.inf)
        far = jnp.float32(SUB)
        cm = row < N_GROUPS
        lc = lt[0:SUB]
        m = jnp.max(jnp.where(cm, lc, neg), axis=0, keepdims=True)
        ec = jnp.exp(jnp.where(cm, lc - m, neg))
        p_grp = 1.0 / jnp.sum(ec, axis=0, keepdims=True)
        grp = jnp.min(jnp.where(cm & (lc == m), row, far), axis=0, keepdims=True)

        lf = lt[SUB:2 * SUB]
        for g in range(1, N_GROUPS):
            lf = jnp.where(grp == g, lt[(g + 1) * SUB:(g + 2) * SUB], lf)
        mf = jnp.max(lf, axis=0, keepdims=True)
        ef = jnp.exp(lf - mf)
        fp = ef / jnp.sum(ef, axis=0, keepdims=True)
        v1 = jnp.max(fp, axis=0, keepdims=True)
        i1 = jnp.min(jnp.where(fp == v1, row, far), axis=0, keepdims=True)
        rest = row != i1
        v2 = jnp.max(jnp.where(rest, fp, -1.0), axis=0, keepdims=True)
        i2 = jnp.min(jnp.where(rest & (fp == v2), row, far), axis=0, keepdims=True)
        den = v1 + v2
        gate = p_grp * jnp.where(row == i1, v1 / den, jnp.where(row == i2, v2 / den, 0.0))

        onehot = (row == grp).astype(F32) * counted
        before = _dot(onehot.astype(BF16), utri_ref[...]) + carry[...]
        rank = jnp.sum(before * onehot, axis=0, keepdims=True)
        carry[...] = carry[...] + jnp.sum(onehot, axis=1, keepdims=True)
        cnt_ref[...] = jnp.broadcast_to(carry[...], (SUB, LANES))
        meta_ref[...] = jnp.where(row == META_GRP, grp, jnp.where(row == META_RANK, rank, 0.0))

        route_ref[...] = jnp.concatenate([gate, jnp.zeros((LANES - SUB, TM), F32)], axis=0).T

    def back_tile_low():
        for c in range(TOK_ROWS // 2):
            hx_ref[_tile_rows(c, TM, TOK_ROWS), :] = h2_s[:, c * LANES:(c + 1) * LANES]

    def back_tile_high():
        for c in range(TOK_ROWS // 2, TOK_ROWS):
            hx_ref[_tile_rows(c, TM, TOK_ROWS), :] = h2_s[:, c * LANES:(c + 1) * LANES]

    back = (back_residual, back_router, back_tile_low, back_tile_high)
    assert len(back) == TM // CHUNK

    for c in range(TM // CHUNK):
        r0 = c * CHUNK
        rows = pl.ds(r0, CHUNK)
        cv = _causal_conv_chunk(ybuf, cw, r0) + convb_ref[...]
        cv = _group_layer_norm(cv, a64, clng_ref[...], clnb_ref[...])
        cat_w[rows, 0:CONV_WIDTH] = (cv * _sigmoid(cv)).astype(BF16)
        for j in range(GMLP_WIDTH // LANES):
            cols = slice(j * LANES, (j + 1) * LANES)
            vj = vn_s[rows, cols]
            rhs = jnp.concatenate([jnp.where(low, vj, 0.0).astype(BF16),
                                   jnp.where(low, 0.0, vj).astype(BF16)], axis=0)
            mixed = _dot(w2_ref[j], rhs) + bfull_ref[:, cols]
            cat_w[rows, CONV_WIDTH + j * LANES:CONV_WIDTH + (j + 1) * LANES] = (
                zu_s[rows, cols] * mixed).astype(BF16)
        back[c]()
    ybuf[0:HALO, :] = ybuf[TM:TM + HALO, :]
    cat_r[...] = cat_w[...]


def _mixer(x, mod3, prm, wg, wu, wd):
    bsz, seq, _ = x.shape
    nt = seq // TM
    n_tiles = bsz * nt
    n_exp = wg.shape[0]
    assert n_exp <= n_tiles + 1
    w_blk = lambda w: pl.BlockSpec((1,) + w.shape[1:], lambda s: (jnp.minimum(s, n_exp - 1), 0, 0))

    def cur(s):
        return jnp.minimum(s, n_tiles - 1)

    def prev(s):
        return jnp.maximum(s - 1, 0)

    const = lambda shape: pl.BlockSpec(shape, lambda s: (0,) * len(shape))
    in_specs = [
        pl.BlockSpec((1, TM, D_MODEL), lambda s: (cur(s) // nt, cur(s) % nt, 0)),
        pl.BlockSpec((1, TM, D_MODEL), lambda s: (prev(s) // nt, prev(s) % nt, 0)),
        pl.BlockSpec((1, SUB, D_MODEL), lambda s: (cur(s) // nt, 0, 0)),
        pl.BlockSpec((1, SUB, D_MODEL), lambda s: (prev(s) // nt, 0, 0)),
        const((1, D_MODEL)),
        const((D_MODEL, 4 * CONV_WIDTH)),
        const((HALO, CONV_WIDTH)),
        const((1, CONV_WIDTH)), const((1, CONV_WIDTH)), const((1, CONV_WIDTH)),
        const((1, GMLP_WIDTH)), const((1, GMLP_WIDTH)),
        const((CONV_WIDTH, CONV_WIDTH)),
        const((GMLP_WIDTH // LANES, CHUNK, 2 * CHUNK)),
        const((CHUNK, GMLP_WIDTH)),
        const((D_MODEL, D_MODEL)),
        const((1, D_MODEL)),
        const((ROUTER_ROWS, D_MODEL)), const((ROUTER_ROWS, D_MODEL)), const((ROUTER_ROWS, 1)),
        const((TM, TM)),
        w_blk(wg), w_blk(wu), w_blk(wd),
    ]
    out_shape = (
        jax.ShapeDtypeStruct((bsz, seq, D_MODEL), F32),
        jax.ShapeDtypeStruct((bsz * seq * TOK_ROWS, LANES), F32),
        jax.ShapeDtypeStruct((bsz * seq, LANES), F32),
        jax.ShapeDtypeStruct((SUB, bsz * seq), F32),
        jax.ShapeDtypeStruct((SUB, LANES), F32),
        jax.ShapeDtypeStruct(wg.shape, BF16),
        jax.ShapeDtypeStruct(wu.shape, BF16),
        jax.ShapeDtypeStruct(wd.shape, BF16),
    )
    out_specs = (
        pl.BlockSpec((1, TM, D_MODEL), lambda s: (prev(s) // nt, prev(s) % nt, 0)),
        pl.BlockSpec((TM * TOK_ROWS, LANES), lambda s: (prev(s), 0)),
        pl.BlockSpec((TM, LANES), lambda s: (prev(s), 0)),
        pl.BlockSpec((SUB, TM), lambda s: (0, prev(s))),
        pl.BlockSpec((SUB, LANES), lambda s: (0, 0)),
        w_blk(wg), w_blk(wu), w_blk(wd),
    )
    return pl.pallas_call(
        functools.partial(_mixer_kernel, tiles_per_seq=nt),
        out_shape=out_shape,
        grid=(n_tiles + 1,),
        in_specs=in_specs,
        out_specs=out_specs,
        scratch_shapes=[
            pltpu.VMEM((TM + HALO, CONV_WIDTH), F32),
            pltpu.VMEM((TM, GMLP_WIDTH), F32),
            pltpu.VMEM((TM, GMLP_WIDTH), F32),
            pltpu.VMEM((TM, D_MODEL), BF16),
            pltpu.VMEM((TM, D_MODEL), BF16),
            pltpu.VMEM((TM, D_MODEL), F32),
            pltpu.VMEM((SUB, 1), F32),
        ],
        compiler_params=pltpu.CompilerParams(
            dimension_semantics=("arbitrary",),
            vmem_limit_bytes=VMEM_LIMIT),
        name="mixer",
    )(x, x, mod3, mod3, *prm, wg, wu, wd)


def _plan_kernel(pos_ref, lo_ref, hi_ref, inv_ref):
    def mark_unused(p, carry):
        inv_ref[p] = jnp.int32(-1)
        return carry

    for g in range(N_GROUPS + 1):
        lax.fori_loop(lo_ref[g], hi_ref[g], mark_unused, 0)

    def place(t, j):
        inv_ref[pos_ref[t]] = t

    _for_each_token(pos_ref.shape[0], place)


def _plan(pos, lo, hi, n_slots):
    return pl.pallas_call(
        _plan_kernel,
        out_shape=jax.ShapeDtypeStruct((n_slots,), jnp.int32),
        grid_spec=pltpu.PrefetchScalarGridSpec(
            num_scalar_prefetch=3,
            grid=(1,),
            in_specs=[],
            out_specs=pl.BlockSpec(memory_space=pltpu.SMEM)),
        compiler_params=pltpu.CompilerParams(dimension_semantics=("arbitrary",)),
        name="plan",
    )(pos, lo, hi)


def _experts_kernel(bg_ref, nv_ref, inv_ref, hx_ref, rtab_ref, wg_ref, wu_ref, wd_ref, yt_ref,
                    xin, rin, yout, xb, rt, xb_nxt, rt_nxt, hid, gsem, ssem, *, n_tokens):
    i = pl.program_id(0)
    nv = nv_ref[0]

    def gather(blk, r, buf):
        tok = jnp.maximum(inv_ref[blk * TB + r], 0)
        rin[buf, pl.ds(r, 1), :] = rtab_ref[pl.ds(tok, 1), :]
        return pltpu.make_async_copy(hx_ref.at[_token_tile(tok, TOK_ROWS), :],
                                     xin.at[buf, _token_tile(r, TOK_ROWS), :], gsem.at[buf])

    def scatter(blk, r, buf):
        tok = inv_ref[jnp.maximum(blk, 0) * TB + r]
        dst = jnp.where((tok < 0) | (blk < 0), n_tokens + buf * TB + r, tok)
        return pltpu.make_async_copy(yout.at[buf, _token_tile(r, SUB), :],
                                     yt_ref.at[_token_tile(dst, SUB), :], ssem.at[buf])

    def wait_gathers(buf):
        pltpu.make_async_copy(hx_ref.at[pl.ds(0, TB * TOK_ROWS), :], xin.at[buf],
                              gsem.at[buf]).wait()

    def wait_scatters(buf):
        pltpu.make_async_copy(yout.at[buf], yt_ref.at[pl.ds(0, TB * SUB), :],
                              ssem.at[buf]).wait()

    def unpack(buf, x_dst, r_dst):
        xs = xin.at[buf]
        for c in range(TOK_ROWS):
            x_dst[:, c * LANES:(c + 1) * LANES] = xs[_tile_rows(c, TB, TOK_ROWS), :].astype(BF16)
        r_dst[...] = rin[buf]

    scur = lax.rem(i, SRING)
    sprv = lax.rem(i + SRING - 1, SRING)
    snxt = lax.rem(i + 1, SRING)

    @pl.when(i == 0)
    def _():
        xin[...] = jnp.zeros(xin.shape, F32)
        rin[...] = jnp.zeros(rin.shape, F32)
        yout[...] = jnp.zeros(yout.shape, F32)
        for b in range(SRING):
            spare = pl.ds((n_tokens + b * TB) * SUB, TB * SUB)
            clear = pltpu.make_async_copy(yout.at[b], yt_ref.at[spare, :], ssem.at[b])
            clear.start()
            clear.wait()
        for b in range(GRING - 1):
            blk = jnp.minimum(b, nv - 1)
            _for_each_token(TB, lambda r, j: gather(blk, r, b).start(priority=j % 2))
        wait_gathers(0)
        unpack(0, xb, rt)

    @pl.when(i < nv)
    def _():
        wait_gathers(lax.rem(i + 1, GRING))

    @pl.when((i < nv) & (i >= SRING - 1))
    def _():
        wait_scatters(scur)

    @pl.when(i < nv)
    def _():
        ahead = jnp.minimum(i + GRING - 1, nv - 1)
        gfar = lax.rem(i + GRING - 1, GRING)
        unpack(lax.rem(i + 1, GRING), xb_nxt, rt_nxt)
        x = xb[...]
        route = rt[...]
        per_expert = TB // EXPERTS_PER_GROUP
        for e in range(EXPERTS_PER_GROUP):
            g = _dot(x, wg_ref[e])
            u = _dot(x, wu_ref[e])
            act = (g * _sigmoid(g)) * u * route[:, e:e + 1]
            hid[:, e * D_EXPERT:(e + 1) * D_EXPERT] = act.astype(BF16)
            for r in range(e * per_expert, (e + 1) * per_expert):
                gather(ahead, r, gfar).start(priority=r % 2)
                scatter(i - 1, r, sprv).start(priority=(r + 1) % 2)
        y = _dot(hid[...], wd_ref[...])
        ys = yout.at[scur]
        for c in range(SUB):
            ys[_tile_rows(c, TB, SUB), :] = y[:, c * LANES:(c + 1) * LANES]
        xb[...] = xb_nxt[...]
        rt[...] = rt_nxt[...]

    @pl.when(i == nv - 1)
    def _():
        _for_each_token(TB, lambda r, j: scatter(i, r, scur).start(priority=j % 2))
        wait_scatters(sprv)
        wait_scatters(scur)
        wait_gathers(lax.rem(i + 2, GRING))
        wait_gathers(lax.rem(i + 3, GRING))

    @pl.when((i == nv - 1) & (i >= 1))
    def _():
        wait_scatters(snxt)


def _experts(blk_grp, n_valid, inv, hx, rtab, wg, wu, wd, n_blocks):
    n_tokens = hx.shape[0] // TOK_ROWS

    def w_map(i, bg, nv, inv_):
        return (bg[i], 0, 0)

    def wd_map(i, bg, nv, inv_):
        return (bg[i], 0)

    return pl.pallas_call(
        functools.partial(_experts_kernel, n_tokens=n_tokens),
        out_shape=jax.ShapeDtypeStruct(((n_tokens + SRING * TB) * SUB, LANES), F32),
        grid_spec=pltpu.PrefetchScalarGridSpec(
            num_scalar_prefetch=3,
            grid=(n_blocks,),
            in_specs=[
                pl.BlockSpec(memory_space=pl.ANY),
                pl.BlockSpec(rtab.shape, lambda i, bg, nv, inv_: (0, 0)),
                pl.BlockSpec((EXPERTS_PER_GROUP, D_MODEL, D_EXPERT), w_map),
                pl.BlockSpec((EXPERTS_PER_GROUP, D_MODEL, D_EXPERT), w_map),
                pl.BlockSpec((EXPERTS_PER_GROUP * D_EXPERT, D_MODEL), wd_map),
            ],
            out_specs=pl.BlockSpec(memory_space=pl.ANY),
            scratch_shapes=[
                pltpu.VMEM((GRING, TB * TOK_ROWS, LANES), F32),
                pltpu.VMEM((GRING, TB, LANES), F32),
                pltpu.VMEM((SRING, TB * SUB, LANES), F32),
                pltpu.VMEM((TB, D_MODEL), BF16),
                pltpu.VMEM((TB, LANES), F32),
                pltpu.VMEM((TB, D_MODEL), BF16),
                pltpu.VMEM((TB, LANES), F32),
                pltpu.VMEM((TB, EXPERTS_PER_GROUP * D_EXPERT), BF16),
                pltpu.SemaphoreType.DMA((GRING,)),
                pltpu.SemaphoreType.DMA((SRING,)),
            ]),
        compiler_params=pltpu.CompilerParams(
            dimension_semantics=("arbitrary",),
            vmem_limit_bytes=VMEM_LIMIT),
        name="experts",
    )(blk_grp, n_valid, inv, hx, rtab, wg, wu, wd)


def _combine_kernel(x1_ref, mod_ref, gfin_ref, yt_ref, o_ref):
    gt2 = mod_ref[0][5:6]
    x2 = []
    ssq = jnp.zeros((TC, 1), F32)
    for c in range(SUB):
        cols = slice(c * LANES, (c + 1) * LANES)
        v = x1_ref[:, cols] + gt2[:, cols] * yt_ref[_tile_rows(c, TC, SUB), :]
        ssq = ssq + jnp.sum(v * v, axis=-1, keepdims=True)
        x2.append(v)
    scale = lax.rsqrt(ssq / D_MODEL + EPS)
    for c in range(SUB):
        cols = slice(c * LANES, (c + 1) * LANES)
        o_ref[:, cols] = x2[c] * scale * gfin_ref[:, cols]


def _combine(x1, mod3, g_final, yt, seq):
    t = x1.shape[0]
    per_batch = seq // TC
    return pl.pallas_call(
        _combine_kernel,
        out_shape=jax.ShapeDtypeStruct((t, D_MODEL), F32),
        grid=(t // TC,),
        in_specs=[
            pl.BlockSpec((TC, D_MODEL), lambda i: (i, 0)),
            pl.BlockSpec((1, SUB, D_MODEL), lambda i: (i // per_batch, 0, 0)),
            pl.BlockSpec((1, D_MODEL), lambda i: (0, 0)),
            pl.BlockSpec((TC * SUB, LANES), lambda i: (i, 0)),
        ],
        out_specs=pl.BlockSpec((TC, D_MODEL), lambda i: (i, 0)),
        compiler_params=pltpu.CompilerParams(
            dimension_semantics=("arbitrary",),
            vmem_limit_bytes=VMEM_LIMIT),
        name="combine",
    )(x1, mod3, g_final, yt)


def _split_bf16(w):
    hi = w.astype(BF16)
    return hi, (w - hi.astype(F32)).astype(BF16)


def kernel(x, c, w_ada, b_ada, g_mix, w_in, conv_w, conv_b, conv_ln_g, conv_ln_b, sgu_ln_g,
           sgu_ln_b, sgu_w, sgu_b, w_out, g_ffn, w_coarse, b_coarse, w_fine, b_fine, w_gate,
           w_up, w_down, g_final):
    bsz, seq, d = x.shape
    t = bsz * seq
    n_blocks = t // TB + N_GROUPS

    mod = _ada(c.T, w_ada[0], b_ada[0][None, :])
    mod3 = jnp.pad(mod.reshape(bsz, 6, d), ((0, 0), (0, 2), (0, 0)))

    tril = jnp.tril(jnp.ones((CHUNK, CHUNK), F32))
    ws = (sgu_w[0] * tril).astype(BF16)
    w2 = jnp.concatenate([ws[0::2], ws[1::2]], axis=2)
    bfull = jnp.repeat(sgu_b[0].T, HEAD_DIM, axis=1)
    ch = jnp.arange(CONV_WIDTH) // HEAD_DIM
    a64 = ((ch[:, None] == ch[None, :]).astype(F32) / HEAD_DIM).astype(BF16)
    convw = jnp.pad(conv_w[0], ((0, HALO - CONV_KSIZE), (0, 0)))
    wr = jnp.zeros((ROUTER_ROWS, d), F32)
    wr = wr.at[:N_GROUPS].set(w_coarse[0].T)
    wr = wr.at[SUB:SUB * (1 + N_GROUPS)].set(
        jnp.transpose(w_fine[0], (0, 2, 1)).reshape(N_GROUPS * EXPERTS_PER_GROUP, d))
    br = jnp.zeros((ROUTER_ROWS, 1), F32)
    br = br.at[:N_GROUPS, 0].set(b_coarse[0])
    br = br.at[SUB:SUB * (1 + N_GROUPS), 0].set(b_fine[0].reshape(-1))
    wrh, wrl = _split_bf16(wr)
    ri = jnp.arange(TM)
    utri = (ri[:, None] < ri[None, :]).astype(BF16)
    prm = (g_mix, w_in[0].astype(BF16), convw, conv_b, conv_ln_g[0].reshape(1, -1),
           conv_ln_b[0].reshape(1, -1), sgu_ln_g[0].reshape(1, -1), sgu_ln_b[0].reshape(1, -1),
           a64, w2, bfull, w_out[0].astype(BF16), g_ffn, wrh, wrl, br, utri)

    x1, hx, rtab, meta, cnt, wg, wu, wd = _mixer(x, mod3, prm, w_gate[0], w_up[0], w_down[0])

    counts = cnt[:N_GROUPS, 0].astype(jnp.int32)
    nblk = (counts + TB - 1) // TB
    blk_end = jnp.cumsum(nblk)
    blk_start = blk_end - nblk
    grp = meta[META_GRP].astype(jnp.int32)
    rank = meta[META_RANK].astype(jnp.int32)
    pos = blk_start[grp] * TB + rank
    n_valid = blk_end[N_GROUPS - 1:]
    unused_lo = jnp.concatenate([blk_start * TB + counts, n_valid * TB])
    unused_hi = jnp.concatenate([blk_end * TB, jnp.full((1,), n_blocks * TB, jnp.int32)])
    blk_ids = jnp.minimum(jnp.arange(n_blocks, dtype=jnp.int32), n_valid[0] - 1)
    blk_grp = jnp.sum((blk_ids[:, None] >= blk_end[None, :]).astype(jnp.int32), axis=1)

    inv = _plan(pos, unused_lo, unused_hi, n_blocks * TB)
    yt = _experts(blk_grp, n_valid, inv, hx, rtab, wg, wu,
                  wd.reshape(N_GROUPS * EXPERTS_PER_GROUP * D_EXPERT, d), n_blocks)
    out = _combine(x1.reshape(t, d), mod3, g_final[None, :], yt, seq)
    return out.reshape(bsz, seq, d)
```

```python
import functools

import jax
import jax.numpy as jnp
import numpy as np
from jax import lax
from jax.experimental import pallas as pl
from jax.experimental.pallas import tpu as pltpu

F32 = jnp.float32
BF16 = jnp.bfloat16

D_MODEL = 1024
CONV_WIDTH = 512
CONV_KSIZE = 31
GMLP_WIDTH = 512
HEAD_DIM = 64
CHUNK = 128
N_GROUPS = 4
EXPERTS_PER_GROUP = 8
D_EXPERT = 256
EPS = 1e-6
assert EXPERTS_PER_GROUP == 8

LANES = 128
SUB = 8
TM = 512
HALO = 32
TB = 256
TC = 1024
TOK_ROWS = D_MODEL // LANES
assert TOK_ROWS == SUB
META_GRP = 0
META_RANK = 1
ROUTER_ROWS = SUB * (1 + N_GROUPS) + SUB
DMA_UNROLL = 8
GRING = 4
SRING = 3
VMEM_LIMIT = 56 * 1024 * 1024


def _dot(a, b):
    return jnp.dot(a, b, preferred_element_type=F32)


def _sigmoid(x):
    return 0.5 * jnp.tanh(0.5 * x) + 0.5


def _gelu_tanh(x):
    c = 0.7978845608028654
    return 0.5 * x * (1.0 + jnp.tanh(c * (x + 0.044715 * (x * x * x))))


def _tile_rows(c, n, per_token):
    return pl.ds(c, n, stride=per_token)


def _token_tile(t, per_token):
    if isinstance(t, int):
        return pl.ds(t * per_token, per_token)
    return pl.ds(pl.multiple_of(t * per_token, per_token), per_token)


def _for_each_token(n, fn):
    def group(gi, carry):
        for j in range(DMA_UNROLL):
            fn(gi * DMA_UNROLL + j, j)
        return carry
    lax.fori_loop(0, n // DMA_UNROLL, group, 0)


def _ada_kernel(ct_ref, w_ref, b_ref, o_ref):
    ct = ct_ref[...]
    ca = ct * _sigmoid(ct)
    w = w_ref[...]
    for b in range(ct.shape[1]):
        o_ref[b:b + 1, :] = jnp.sum(w * ca[:, b:b + 1], axis=0, keepdims=True) + b_ref[...]


def _ada(c_t, w_ada, b_ada):
    n = w_ada.shape[1]
    bsz = c_t.shape[1]
    tn = 1024
    return pl.pallas_call(
        _ada_kernel,
        out_shape=jax.ShapeDtypeStruct((bsz, n), F32),
        grid=(n // tn,),
        in_specs=[
            pl.BlockSpec((D_MODEL, bsz), lambda j: (0, 0)),
            pl.BlockSpec((D_MODEL, tn), lambda j: (0, j)),
            pl.BlockSpec((1, tn), lambda j: (0, j)),
        ],
        out_specs=pl.BlockSpec((bsz, tn), lambda j: (0, j)),
        name="ada",
    )(c_t, w_ada, b_ada)


def _group_layer_norm(v, a64, g, b):
    mu = _dot(v.astype(BF16), a64)
    d = v - mu
    var = _dot((d * d).astype(BF16), a64)
    return d * lax.rsqrt(var + EPS) * g + b


def _causal_conv_chunk(ybuf, cw, r0):
    out = None
    for r in range(SUB):
        part = None
        for q in range((CONV_KSIZE - 1 - r) // SUB + 1):
            k = CONV_KSIZE - 1 - (SUB * q + r)
            seg = ybuf[pl.ds(r0 + HALO - SUB - SUB * q, CHUNK + SUB), :]
            term = seg * cw[k:k + 1, :]
            part = term if part is None else part + term
        shifted = part[SUB - r:SUB - r + CHUNK, :]
        out = shifted if out is None else out + shifted
    return out


def _mixer_kernel(x_ref, xp_ref, mod_ref, modp_ref, gmix_ref, win_ref, convw_ref, convb_ref,
                  clng_ref, clnb_ref, slng_ref, slnb_ref, a64_ref, w2_ref, bfull_ref, wout_ref,
                  gffn_ref, wrh_ref, wrl_ref, br_ref, utri_ref, wg_ref, wu_ref, wd_ref,
                  x1_ref, hx_ref, route_ref, meta_ref, cnt_ref, wgb_ref, wub_ref, wdb_ref,
                  ybuf, zu_s, vn_s, cat_w, cat_r, h2_s, carry, *, tiles_per_seq):
    s = pl.program_id(0)
    n_tiles = pl.num_programs(0) - 1

    @pl.when(lax.rem(jnp.minimum(s, n_tiles - 1), tiles_per_seq) == 0)
    def _():
        ybuf[0:HALO, :] = jnp.zeros((HALO, CONV_WIDTH), F32)

    @pl.when(s == 0)
    def _():
        carry[...] = jnp.zeros((SUB, 1), F32)
        cat_r[...] = jnp.zeros((TM, D_MODEL), BF16)

    counted = jnp.where(s > 0, 1.0, 0.0)

    wgb_ref[...] = wg_ref[...].astype(BF16)
    wub_ref[...] = wu_ref[...].astype(BF16)
    wdb_ref[...] = wd_ref[...].astype(BF16)

    x = x_ref[0]
    mod = mod_ref[0]
    sh1, sc1 = mod[0:1], mod[1:2]

    ms = jnp.mean(x * x, axis=-1, keepdims=True)
    h = x * lax.rsqrt(ms + EPS) * (gmix_ref[...] * (1.0 + sc1)) + sh1
    hb = h.astype(BF16)
    a64 = a64_ref[...]

    ga = _dot(hb, win_ref[:, 0:CONV_WIDTH])
    gg = _dot(hb, win_ref[:, CONV_WIDTH:2 * CONV_WIDTH])
    ybuf[HALO:HALO + TM, :] = ga * _sigmoid(gg)
    off = 2 * CONV_WIDTH
    zu_s[...] = _gelu_tanh(_dot(hb, win_ref[:, off:off + GMLP_WIDTH]))
    zv = _gelu_tanh(_dot(hb, win_ref[:, off + GMLP_WIDTH:off + 2 * GMLP_WIDTH]))
    vn_s[...] = _group_layer_norm(zv, a64, slng_ref[...], slnb_ref[...])

    cw = convw_ref[...]
    low = lax.broadcasted_iota(jnp.int32, (CHUNK, LANES), 1) < HEAD_DIM

    def back_residual():
        gt1 = modp_ref[0][2:3]
        x1_ref[0] = xp_ref[0] + gt1 * _dot(cat_r[...], wout_ref[...])

    def back_router():
        modp = modp_ref[0]
        sh2, sc2 = modp[3:4], modp[4:5]
        x1 = x1_ref[0]
        ms2 = jnp.mean(x1 * x1, axis=-1, keepdims=True)
        h2 = x1 * lax.rsqrt(ms2 + EPS) * (gffn_ref[...] * (1.0 + sc2)) + sh2
        h2_s[...] = h2
        hh = h2.astype(BF16)
        hl = (h2 - hh.astype(F32)).astype(BF16)

        def nt_dot(w, a):
            return lax.dot_general(w, a, (((1,), (1,)), ((), ())), preferred_element_type=F32)

        lt = (nt_dot(wrh_ref[...], hh) + nt_dot(wrh_ref[...], hl) + nt_dot(wrl_ref[...], hh)
              + br_ref[...])
        row = lax.broadcasted_iota(jnp.int32, (SUB, TM), 0).astype(F32)
        neg = jnp.float32(-jnp.inf)
        far = jnp.float32(SUB)
        cm = row < N_GROUPS
        lc = lt[0:SUB]
        m = jnp.max(jnp.where(cm, lc, neg), axis=0, keepdims=True)
        ec = jnp.exp(jnp.where(cm, lc - m, neg))
        p_grp = 1.0 / jnp.sum(ec, axis=0, keepdims=True)
        grp = jnp.min(jnp.where(cm & (lc == m), row, far), axis=0, keepdims=True)

        lf = lt[SUB:2 * SUB]
        for g in range(1, N_GROUPS):
            lf = jnp.where(grp == g, lt[(g + 1) * SUB:(g + 2) * SUB], lf)
        mf = jnp.max(lf, axis=0, keepdims=True)
        ef = jnp.exp(lf - mf)
        fp = ef / jnp.sum(ef, axis=0, keepdims=True)
        v1 = jnp.max(fp, axis=0, keepdims=True)
        i1 = jnp.min(jnp.where(fp == v1, row, far), axis=0, keepdims=True)
        rest = row != i1
        v2 = jnp.max(jnp.where(rest, fp, -1.0), axis=0, keepdims=True)
        i2 = jnp.min(jnp.where(rest & (fp == v2), row, far), axis=0, keepdims=True)
        den = v1 + v2
        gate = p_grp * jnp.where(row == i1, v1 / den, jnp.where(row == i2, v2 / den, 0.0))

        onehot = (row == grp).astype(F32) * counted
        before = _dot(onehot.astype(BF16), utri_ref[...]) + carry[...]
        rank = jnp.sum(before * onehot, axis=0, keepdims=True)
        carry[...] = carry[...] + jnp.sum(onehot, axis=1, keepdims=True)
        cnt_ref[...] = jnp.broadcast_to(carry[...], (SUB, LANES))
        meta_ref[...] = jnp.where(row == META_GRP, grp, jnp.where(row == META_RANK, rank, 0.0))

        route_ref[...] = jnp.concatenate([gate, jnp.zeros((LANES - SUB, TM), F32)], axis=0).T

    def back_tile_low():
        for c in range(TOK_ROWS // 2):
            hx_ref[_tile_rows(c, TM, TOK_ROWS), :] = h2_s[:, c * LANES:(c + 1) * LANES]

    def back_tile_high():
        for c in range(TOK_ROWS // 2, TOK_ROWS):
            hx_ref[_tile_rows(c, TM, TOK_ROWS), :] = h2_s[:, c * LANES:(c + 1) * LANES]

    back = (back_residual, back_router, back_tile_low, back_tile_high)
    assert len(back) == TM // CHUNK

    for c in range(TM // CHUNK):
        r0 = c * CHUNK
        rows = pl.ds(r0, CHUNK)
        cv = _causal_conv_chunk(ybuf, cw, r0) + convb_ref[...]
        cv = _group_layer_norm(cv, a64, clng_ref[...], clnb_ref[...])
        cat_w[rows, 0:CONV_WIDTH] = (cv * _sigmoid(cv)).astype(BF16)
        for j in range(GMLP_WIDTH // LANES):
            cols = slice(j * LANES, (j + 1) * LANES)
            vj = vn_s[rows, cols]
            rhs = jnp.concatenate([jnp.where(low, vj, 0.0).astype(BF16),
                                   jnp.where(low, 0.0, vj).astype(BF16)], axis=0)
            mixed = _dot(w2_ref[j], rhs) + bfull_ref[:, cols]
            cat_w[rows, CONV_WIDTH + j * LANES:CONV_WIDTH + (j + 1) * LANES] = (
                zu_s[rows, cols] * mixed).astype(BF16)
        back[c]()
    ybuf[0:HALO, :] = ybuf[TM:TM + HALO, :]
    cat_r[...] = cat_w[...]


def _mixer(x, mod3, prm, wg, wu, wd):
    bsz, seq, _ = x.shape
    nt = seq // TM
    n_tiles = bsz * nt
    n_exp = wg.shape[0]
    assert n_exp <= n_tiles + 1
    w_blk = lambda w: pl.BlockSpec((1,) + w.shape[1:], lambda s: (jnp.minimum(s, n_exp - 1), 0, 0))

    def cur(s):
        return jnp.minimum(s, n_tiles - 1)

    def prev(s):
        return jnp.maximum(s - 1, 0)

    const = lambda shape: pl.BlockSpec(shape, lambda s: (0,) * len(shape))
    in_specs = [
        pl.BlockSpec((1, TM, D_MODEL), lambda s: (cur(s) // nt, cur(s) % nt, 0)),
        pl.BlockSpec((1, TM, D_MODEL), lambda s: (prev(s) // nt, prev(s) % nt, 0)),
        pl.BlockSpec((1, SUB, D_MODEL), lambda s: (cur(s) // nt, 0, 0)),
        pl.BlockSpec((1, SUB, D_MODEL), lambda s: (prev(s) // nt, 0, 0)),
        const((1, D_MODEL)),
        const((D_MODEL, 4 * CONV_WIDTH)),
        const((HALO, CONV_WIDTH)),
        const((1, CONV_WIDTH)), const((1, CONV_WIDTH)), const((1, CONV_WIDTH)),
        const((1, GMLP_WIDTH)), const((1, GMLP_WIDTH)),
        const((CONV_WIDTH, CONV_WIDTH)),
        const((GMLP_WIDTH // LANES, CHUNK, 2 * CHUNK)),
        const((CHUNK, GMLP_WIDTH)),
        const((D_MODEL, D_MODEL)),
        const((1, D_MODEL)),
        const((ROUTER_ROWS, D_MODEL)), const((ROUTER_ROWS, D_MODEL)), const((ROUTER_ROWS, 1)),
        const((TM, TM)),
        w_blk(wg), w_blk(wu), w_blk(wd),
    ]
    out_shape = (
        jax.ShapeDtypeStruct((bsz, seq, D_MODEL), F32),
        jax.ShapeDtypeStruct((bsz * seq * TOK_ROWS, LANES), F32),
        jax.ShapeDtypeStruct((bsz * seq, LANES), F32),
        jax.ShapeDtypeStruct((SUB, bsz * seq), F32),
        jax.ShapeDtypeStruct((SUB, LANES), F32),
        jax.ShapeDtypeStruct(wg.shape, BF16),
        jax.ShapeDtypeStruct(wu.shape, BF16),
        jax.ShapeDtypeStruct(wd.shape, BF16),
    )
    out_specs = (
        pl.BlockSpec((1, TM, D_MODEL), lambda s: (prev(s) // nt, prev(s) % nt, 0)),
        pl.BlockSpec((TM * TOK_ROWS, LANES), lambda s: (prev(s), 0)),
        pl.BlockSpec((TM, LANES), lambda s: (prev(s), 0)),
        pl.BlockSpec((SUB, TM), lambda s: (0, prev(s))),
        pl.BlockSpec((SUB, LANES), lambda s: (0, 0)),
        w_blk(wg), w_blk(wu), w_blk(wd),
    )
    return pl.pallas_call(
        functools.partial(_mixer_kernel, tiles_per_seq=nt),
        out_shape=out_shape,
        grid=(n_tiles + 1,),
        in_specs=in_specs,
        out_specs=out_specs,
        scratch_shapes=[
            pltpu.VMEM((TM + HALO, CONV_WIDTH), F32),
            pltpu.VMEM((TM, GMLP_WIDTH), F32),
            pltpu.VMEM((TM, GMLP_WIDTH), F32),
            pltpu.VMEM((TM, D_MODEL), BF16),
            pltpu.VMEM((TM, D_MODEL), BF16),
            pltpu.VMEM((TM, D_MODEL), F32),
            pltpu.VMEM((SUB, 1), F32),
        ],
        compiler_params=pltpu.CompilerParams(
            dimension_semantics=("arbitrary",),
            vmem_limit_bytes=VMEM_LIMIT),
        name="mixer",
    )(x, x, mod3, mod3, *prm, wg, wu, wd)


def _experts_kernel(bg_ref, nv_ref, pos_ref, lo_ref, hi_ref, hx_ref, rtab_ref, wg_ref, wu_ref,
                    wd_ref, yt_ref,
                    inv_ref, xin, rin, yout, xb, rt, xb_nxt, rt_nxt, hid, gsem, ssem, *, n_tokens):
    i = pl.program_id(0)
    nv = nv_ref[0]

    def gather(blk, r, buf):
        tok = jnp.maximum(inv_ref[blk * TB + r], 0)
        rin[buf, pl.ds(r, 1), :] = rtab_ref[pl.ds(tok, 1), :]
        return pltpu.make_async_copy(hx_ref.at[_token_tile(tok, TOK_ROWS), :],
                                     xin.at[buf, _token_tile(r, TOK_ROWS), :], gsem.at[buf])

    def scatter(blk, r, buf):
        tok = inv_ref[jnp.maximum(blk, 0) * TB + r]
        dst = jnp.where((tok < 0) | (blk < 0), n_tokens + buf * TB + r, tok)
        return pltpu.make_async_copy(yout.at[buf, _token_tile(r, SUB), :],
                                     yt_ref.at[_token_tile(dst, SUB), :], ssem.at[buf])

    def wait_gathers(buf):
        pltpu.make_async_copy(hx_ref.at[pl.ds(0, TB * TOK_ROWS), :], xin.at[buf],
                              gsem.at[buf]).wait()

    def wait_scatters(buf):
        pltpu.make_async_copy(yout.at[buf], yt_ref.at[pl.ds(0, TB * SUB), :],
                              ssem.at[buf]).wait()

    def unpack(buf, x_dst, r_dst):
        xs = xin.at[buf]
        for c in range(TOK_ROWS):
            x_dst[:, c * LANES:(c + 1) * LANES] = xs[_tile_rows(c, TB, TOK_ROWS), :].astype(BF16)
        r_dst[...] = rin[buf]

    scur = lax.rem(i, SRING)
    sprv = lax.rem(i + SRING - 1, SRING)
    snxt = lax.rem(i + 1, SRING)

    @pl.when(i == 0)
    def _():
        def mark_unused(p, carry):
            inv_ref[p] = jnp.int32(-1)
            return carry

        for g in range(N_GROUPS + 1):
            lax.fori_loop(lo_ref[g], hi_ref[g], mark_unused, 0)

        def place(t, j):
            inv_ref[pos_ref[t]] = t

        _for_each_token(n_tokens, place)

        xin[...] = jnp.zeros(xin.shape, F32)
        rin[...] = jnp.zeros(rin.shape, F32)
        yout[...] = jnp.zeros(yout.shape, F32)
        for b in range(SRING):
            spare = pl.ds((n_tokens + b * TB) * SUB, TB * SUB)
            clear = pltpu.make_async_copy(yout.at[b], yt_ref.at[spare, :], ssem.at[b])
            clear.start()
            clear.wait()
        for b in range(GRING - 1):
            blk = jnp.minimum(b, nv - 1)
            _for_each_token(TB, lambda r, j: gather(blk, r, b).start(priority=j % 2))
        wait_gathers(0)
        unpack(0, xb, rt)

    @pl.when(i < nv)
    def _():
        wait_gathers(lax.rem(i + 1, GRING))

    @pl.when((i < nv) & (i >= SRING - 1))
    def _():
        wait_scatters(scur)

    @pl.when(i < nv)
    def _():
        ahead = jnp.minimum(i + GRING - 1, nv - 1)
        gfar = lax.rem(i + GRING - 1, GRING)
        unpack(lax.rem(i + 1, GRING), xb_nxt, rt_nxt)
        x = xb[...]
        route = rt[...]
        per_expert = TB // EXPERTS_PER_GROUP
        for e in range(EXPERTS_PER_GROUP):
            g = _dot(x, wg_ref[e])
            u = _dot(x, wu_ref[e])
            act = (g * _sigmoid(g)) * u * route[:, e:e + 1]
            hid[:, e * D_EXPERT:(e + 1) * D_EXPERT] = act.astype(BF16)
            for r in range(e * per_expert, (e + 1) * per_expert):
                gather(ahead, r, gfar).start(priority=r % 2)
                scatter(i - 1, r, sprv).start(priority=(r + 1) % 2)
        y = _dot(hid[...], wd_ref[...])
        ys = yout.at[scur]
        for c in range(SUB):
            ys[_tile_rows(c, TB, SUB), :] = y[:, c * LANES:(c + 1) * LANES]
        xb[...] = xb_nxt[...]
        rt[...] = rt_nxt[...]

    @pl.when(i == nv - 1)
    def _():
        _for_each_token(TB, lambda r, j: scatter(i, r, scur).start(priority=j % 2))
        wait_scatters(sprv)
        wait_scatters(scur)
        wait_gathers(lax.rem(i + 2, GRING))
        wait_gathers(lax.rem(i + 3, GRING))

    @pl.when((i == nv - 1) & (i >= 1))
    def _():
        wait_scatters(snxt)


def _experts(blk_grp, n_valid, pos, lo, hi, hx, rtab, wg, wu, wd, n_blocks):
    n_tokens = hx.shape[0] // TOK_ROWS

    def w_map(i, bg, *_):
        return (bg[i], 0, 0)

    def wd_map(i, bg, *_):
        return (bg[i], 0)

    return pl.pallas_call(
        functools.partial(_experts_kernel, n_tokens=n_tokens),
        out_shape=jax.ShapeDtypeStruct(((n_tokens + SRING * TB) * SUB, LANES), F32),
        grid_spec=pltpu.PrefetchScalarGridSpec(
            num_scalar_prefetch=5,
            grid=(n_blocks,),
            in_specs=[
                pl.BlockSpec(memory_space=pl.ANY),
                pl.BlockSpec(rtab.shape, lambda i, *_: (0, 0)),
                pl.BlockSpec((EXPERTS_PER_GROUP, D_MODEL, D_EXPERT), w_map),
                pl.BlockSpec((EXPERTS_PER_GROUP, D_MODEL, D_EXPERT), w_map),
                pl.BlockSpec((EXPERTS_PER_GROUP * D_EXPERT, D_MODEL), wd_map),
            ],
            out_specs=pl.BlockSpec(memory_space=pl.ANY),
            scratch_shapes=[
                pltpu.SMEM((n_blocks * TB,), jnp.int32),
                pltpu.VMEM((GRING, TB * TOK_ROWS, LANES), F32),
                pltpu.VMEM((GRING, TB, LANES), F32),
                pltpu.VMEM((SRING, TB * SUB, LANES), F32),
                pltpu.VMEM((TB, D_MODEL), BF16),
                pltpu.VMEM((TB, LANES), F32),
                pltpu.VMEM((TB, D_MODEL), BF16),
                pltpu.VMEM((TB, LANES), F32),
                pltpu.VMEM((TB, EXPERTS_PER_GROUP * D_EXPERT), BF16),
                pltpu.SemaphoreType.DMA((GRING,)),
                pltpu.SemaphoreType.DMA((SRING,)),
            ]),
        compiler_params=pltpu.CompilerParams(
            dimension_semantics=("arbitrary",),
            vmem_limit_bytes=VMEM_LIMIT),
        name="experts",
    )(blk_grp, n_valid, pos, lo, hi, hx, rtab, wg, wu, wd)


def _combine_kernel(x1_ref, mod_ref, gfin_ref, yt_ref, o_ref):
    gt2 = mod_ref[0][5:6]
    x2 = []
    ssq = jnp.zeros((TC, 1), F32)
    for c in range(SUB):
        cols = slice(c * LANES, (c + 1) * LANES)
        v = x1_ref[:, cols] + gt2[:, cols] * yt_ref[_tile_rows(c, TC, SUB), :]
        ssq = ssq + jnp.sum(v * v, axis=-1, keepdims=True)
        x2.append(v)
    scale = lax.rsqrt(ssq / D_MODEL + EPS)
    for c in range(SUB):
        cols = slice(c * LANES, (c + 1) * LANES)
        o_ref[:, cols] = x2[c] * scale * gfin_ref[:, cols]


def _combine(x1, mod3, g_final, yt, seq):
    t = x1.shape[0]
    per_batch = seq // TC
    return pl.pallas_call(
        _combine_kernel,
        out_shape=jax.ShapeDtypeStruct((t, D_MODEL), F32),
        grid=(t // TC,),
        in_specs=[
            pl.BlockSpec((TC, D_MODEL), lambda i: (i, 0)),
            pl.BlockSpec((1, SUB, D_MODEL), lambda i: (i // per_batch, 0, 0)),
            pl.BlockSpec((1, D_MODEL), lambda i: (0, 0)),
            pl.BlockSpec((TC * SUB, LANES), lambda i: (i, 0)),
        ],
        out_specs=pl.BlockSpec((TC, D_MODEL), lambda i: (i, 0)),
        compiler_params=pltpu.CompilerParams(
            dimension_semantics=("arbitrary",),
            vmem_limit_bytes=VMEM_LIMIT),
        name="combine",
    )(x1, mod3, g_final, yt)


def _split_bf16(w):
    hi = w.astype(BF16)
    return hi, (w - hi.astype(F32)).astype(BF16)


def kernel(x, c, w_ada, b_ada, g_mix, w_in, conv_w, conv_b, conv_ln_g, conv_ln_b, sgu_ln_g,
           sgu_ln_b, sgu_w, sgu_b, w_out, g_ffn, w_coarse, b_coarse, w_fine, b_fine, w_gate,
           w_up, w_down, g_final):
    bsz, seq, d = x.shape
    t = bsz * seq
    n_blocks = t // TB + N_GROUPS

    mod = _ada(c.T, w_ada[0], b_ada[0][None, :])
    mod3 = jnp.pad(mod.reshape(bsz, 6, d), ((0, 0), (0, 2), (0, 0)))

    tril = np.tril(np.ones((CHUNK, CHUNK), np.float32))
    ws = (sgu_w[0] * tril).astype(BF16)
    w2 = jnp.concatenate([ws[0::2], ws[1::2]], axis=2)
    bfull = jnp.repeat(sgu_b[0].T, HEAD_DIM, axis=1)
    ch = np.arange(CONV_WIDTH) // HEAD_DIM
    a64 = jnp.asarray((ch[:, None] == ch[None, :]).astype(np.float32) / HEAD_DIM, dtype=BF16)
    convw = jnp.pad(conv_w[0], ((0, HALO - CONV_KSIZE), (0, 0)))
    n_fine = N_GROUPS * EXPERTS_PER_GROUP
    wr = jnp.concatenate([
        w_coarse[0].T, jnp.zeros((SUB - N_GROUPS, d), F32),
        jnp.transpose(w_fine[0], (0, 2, 1)).reshape(n_fine, d),
        jnp.zeros((ROUTER_ROWS - SUB - n_fine, d), F32)], axis=0)
    br = jnp.concatenate([
        b_coarse[0], jnp.zeros((SUB - N_GROUPS,), F32), b_fine[0].reshape(-1),
        jnp.zeros((ROUTER_ROWS - SUB - n_fine,), F32)])[:, None]
    wrh, wrl = _split_bf16(wr)
    ri = np.arange(TM)
    utri = jnp.asarray((ri[:, None] < ri[None, :]).astype(np.float32), dtype=BF16)
    prm = (g_mix, w_in[0].astype(BF16), convw, conv_b, conv_ln_g[0].reshape(1, -1),
           conv_ln_b[0].reshape(1, -1), sgu_ln_g[0].reshape(1, -1), sgu_ln_b[0].reshape(1, -1),
           a64, w2, bfull, w_out[0].astype(BF16), g_ffn, wrh, wrl, br, utri)

    x1, hx, rtab, meta, cnt, wg, wu, wd = _mixer(x, mod3, prm, w_gate[0], w_up[0], w_down[0])

    counts = cnt[:N_GROUPS, 0].astype(jnp.int32)
    nblk = (counts + TB - 1) // TB
    blk_end = jnp.cumsum(nblk)
    blk_start = blk_end - nblk
    grp = meta[META_GRP].astype(jnp.int32)
    rank = meta[META_RANK].astype(jnp.int32)
    pos = blk_start[grp] * TB + rank
    n_valid = blk_end[N_GROUPS - 1:]
    unused_lo = jnp.concatenate([blk_start * TB + counts, n_valid * TB])
    unused_hi = jnp.concatenate([blk_end * TB, jnp.full((1,), n_blocks * TB, jnp.int32)])
    blk_ids = jnp.minimum(jnp.arange(n_blocks, dtype=jnp.int32), n_valid[0] - 1)
    blk_grp = jnp.sum((blk_ids[:, None] >= blk_end[None, :]).astype(jnp.int32), axis=1)

    yt = _experts(blk_grp, n_valid, pos, unused_lo, unused_hi, hx, rtab, wg, wu,
                  wd.reshape(N_GROUPS * EXPERTS_PER_GROUP * D_EXPERT, d), n_blocks)
    out = _combine(x1.reshape(t, d), mod3, g_final[None, :], yt, seq)
    return out.reshape(bsz, seq, d)
```
